```python
import jax, jax.numpy as jnp
from jax import lax
import numpy as np

D_MODEL = 2048
BATCH = 8
SEQ = 2048
DEPTH = 1

CHUNK = 64
D_MIX = D_MODEL
HGRN_WIDTH = D_MIX // 2
HGRN_HEAD_DIM = 128
HGRN_HEADS = HGRN_WIDTH // HGRN_HEAD_DIM
CONV_CH = D_MIX - HGRN_WIDTH
CONV_K = 3
D_FF = 5632
FFN_RESIDUAL_SCALE = 0.5
RMS_EPS = 1e-6
IN_COLS = 4 * HGRN_WIDTH + 3 * CONV_CH

kernel_name = "hgrn2_shortconv_macaron_hybrid"


def rmsnorm(x, w):
    xf = x.astype(jnp.float32)
    y = xf * lax.rsqrt(jnp.mean(xf * xf, axis=-1, keepdims=True) + RMS_EPS)
    return (y * w.astype(jnp.float32)).astype(x.dtype)


def swiglu(h, w_gate, w_up, w_down):
    return (jax.nn.silu(h @ w_gate) * (h @ w_up)) @ w_down


def hgrn2_chunkwise(q, f_logit, i_val, lb):
    bsz, seq, heads, dh = q.shape
    n_chunks = seq // CHUNK
    lb = lb.reshape(heads, dh)
    f = lb + (1.0 - lb) * jax.nn.sigmoid(f_logit)
    log_f = jnp.log(f)
    k = (1.0 - lb) * jax.nn.sigmoid(-f_logit)

    def to_chunks(t):
        return t.reshape(bsz, n_chunks, CHUNK, heads, dh).transpose(1, 0, 3, 2, 4)

    q_c, k_c, v_c, lf_c = to_chunks(q), to_chunks(k), to_chunks(i_val), to_chunks(log_f)
    b = jnp.cumsum(lf_c, axis=3)
    b_last = b[:, :, :, -1:, :]
    q_dec = q_c * jnp.exp(b)
    k_dec = k_c * jnp.exp(-b)
    k_state = k_c * jnp.exp(b_last - b)
    chunk_decay = jnp.exp(b_last[:, :, :, 0, :])

    causal = jnp.tril(jnp.ones((CHUNK, CHUNK), dtype=bool))
    scores = jnp.einsum('nbhck,nbhsk->nbhcs', q_dec, k_dec)
    scores = jnp.where(causal, scores, 0.0)
    o_intra = jnp.einsum('nbhcs,nbhsv->nbhcv', scores, v_c)

    def step(state, inp):
        q_n, k_n, v_n, d_n = inp
        o_n = jnp.einsum('bhck,bhkv->bhcv', q_n, state)
        state = state * d_n[..., None] + jnp.einsum('bhck,bhcv->bhkv', k_n, v_n)
        return state, o_n

    state0 = jnp.zeros((bsz, heads, dh, dh), jnp.float32)
    _, o_inter = lax.scan(step, state0, (q_dec, k_state, v_c, chunk_decay))
    o = o_intra + o_inter
    return o.transpose(1, 0, 3, 2, 4).reshape(bsz, seq, heads, dh)


def causal_depthwise_conv(u, w):
    ch = u.shape[-1]
    rhs = w.reshape(CONV_K, 1, ch).astype(u.dtype)
    return lax.conv_general_dilated(u, rhs, window_strides=(1,), padding=[(CONV_K - 1, 0)],
                                    dimension_numbers=('NWC', 'WIO', 'NWC'), feature_group_count=ch)


def hybrid_mixer(h, w_in, lb, out_norm_w, conv_w, w_out):
    bsz, seq, _ = h.shape
    proj = h @ w_in
    W, C = HGRN_WIDTH, CONV_CH
    q, f_logit, i_val, gate, conv_b, conv_c, conv_h = jnp.split(
        proj, [W, 2 * W, 3 * W, 4 * W, 4 * W + C, 4 * W + 2 * C], axis=-1)

    hs = (bsz, seq, HGRN_HEADS, HGRN_HEAD_DIM)
    o = hgrn2_chunkwise(q.astype(jnp.float32).reshape(hs), f_logit.astype(jnp.float32).reshape(hs),
                        i_val.astype(jnp.float32).reshape(hs), lb)
    o = o * lax.rsqrt(jnp.mean(o * o, axis=-1, keepdims=True) + RMS_EPS)
    o = o * out_norm_w.astype(jnp.float32).reshape(HGRN_HEADS, HGRN_HEAD_DIM)
    o_a = (o.reshape(bsz, seq, W) * jax.nn.silu(gate.astype(jnp.float32))).astype(h.dtype)

    o_b = conv_b * causal_depthwise_conv(conv_c * conv_h, conv_w)

    return jnp.concatenate([o_a, o_b], axis=-1) @ w_out


def setup_inputs(seed: int = 0) -> dict:
    key = jax.random.key(seed)
    ks = jax.random.split(key, 17)
    f32 = jnp.float32

    def dense(k, shape, fan_in):
        return jax.random.normal(k, shape, f32) * (fan_in ** -0.5)

    def gain(k, shape):
        return 1.0 + 0.02 * jax.random.normal(k, shape, f32)

    return {
        "x": jax.random.normal(ks[0], (BATCH, SEQ, D_MODEL), f32),
        "ffn1_norm_w": gain(ks[1], (DEPTH, D_MODEL)),
        "ffn1_w_gate": dense(ks[2], (DEPTH, D_MODEL, D_FF), D_MODEL),
        "ffn1_w_up": dense(ks[3], (DEPTH, D_MODEL, D_FF), D_MODEL),
        "ffn1_w_down": dense(ks[4], (DEPTH, D_FF, D_MODEL), D_FF),
        "mix_norm_w": gain(ks[5], (DEPTH, D_MODEL)),
        "w_in_mix": dense(ks[6], (DEPTH, D_MODEL, IN_COLS), D_MODEL),
        "hgrn_lb_logits": 0.1 * jax.random.normal(ks[7], (DEPTH + 1, HGRN_WIDTH), f32),
        "hgrn_out_norm_w": gain(ks[8], (DEPTH, HGRN_WIDTH)),
        "conv_w": dense(ks[9], (DEPTH, CONV_K, CONV_CH), CONV_K),
        "w_out_mix": dense(ks[10], (DEPTH, D_MIX, D_MODEL), D_MIX),
        "ffn2_norm_w": gain(ks[11], (DEPTH, D_MODEL)),
        "ffn2_w_gate": dense(ks[12], (DEPTH, D_MODEL, D_FF), D_MODEL),
        "ffn2_w_up": dense(ks[13], (DEPTH, D_MODEL, D_FF), D_MODEL),
        "ffn2_w_down": dense(ks[14], (DEPTH, D_FF, D_MODEL), D_FF),
        "final_norm_w": gain(ks[15], (D_MODEL,)),
    }


def reference(x, ffn1_norm_w, ffn1_w_gate, ffn1_w_up, ffn1_w_down, mix_norm_w, w_in_mix,
              hgrn_lb_logits, hgrn_out_norm_w, conv_w, w_out_mix, ffn2_norm_w, ffn2_w_gate,
              ffn2_w_up, ffn2_w_down, final_norm_w):
    lower_bounds = jnp.cumsum(jax.nn.softmax(hgrn_lb_logits.astype(jnp.float32), axis=0), axis=0)
    for l in range(DEPTH):
        x = x + FFN_RESIDUAL_SCALE * swiglu(rmsnorm(x, ffn1_norm_w[l]),
                                            ffn1_w_gate[l], ffn1_w_up[l], ffn1_w_down[l])
        x = x + hybrid_mixer(rmsnorm(x, mix_norm_w[l]), w_in_mix[l], lower_bounds[l],
                             hgrn_out_norm_w[l], conv_w[l], w_out_mix[l])
        x = x + FFN_RESIDUAL_SCALE * swiglu(rmsnorm(x, ffn2_norm_w[l]),
                                            ffn2_w_gate[l], ffn2_w_up[l], ffn2_w_down[l])
    return rmsnorm(x, final_norm_w)
```

```python
import functools

import jax
import jax.numpy as jnp
from jax import lax
from jax.experimental import pallas as pl
from jax.experimental.pallas import tpu as pltpu

LANES = 128
CHUNK = 64
HEAD_DIM = 128
CONV_K = 3
RMS_EPS = 1e-6
FFN_RESIDUAL_SCALE = 0.5
MIB = 1024 * 1024

F32 = jnp.float32
BF16 = jnp.bfloat16


def _rmsnorm_rows(x, w):
    return x * lax.rsqrt(jnp.mean(x * x, axis=-1, keepdims=True) + RMS_EPS) * w


def _ffn_kernel(x_ref, nw_ref, wg_ref, wu_ref, wd_ref, fw_ref, o_ref, h_ref, *, final_norm):
    j = pl.program_id(1)

    @pl.when(j == 0)
    def _():
        h_ref[...] = _rmsnorm_rows(x_ref[...], nw_ref[...]).astype(BF16)
        o_ref[...] = jnp.zeros_like(o_ref)

    h = h_ref[...]
    g = jnp.dot(h, wg_ref[...], preferred_element_type=F32)
    u = jnp.dot(h, wu_ref[...], preferred_element_type=F32)
    a = (g * (1.0 / (1.0 + jnp.exp(-g)))) * u
    o_ref[...] += jnp.dot(a.astype(BF16), wd_ref[...], preferred_element_type=F32)

    @pl.when(j == pl.num_programs(1) - 1)
    def _():
        y = x_ref[...] + FFN_RESIDUAL_SCALE * o_ref[...]
        if final_norm:
            y = _rmsnorm_rows(y, fw_ref[...])
        o_ref[...] = y


def _ffn(x, norm_w, w_gate, w_up, w_down, final_w, *, final_norm, tm=512, tf=512):
    m, d = x.shape
    d_ff = w_gate.shape[1]
    assert m % tm == 0 and d_ff % tf == 0
    return pl.pallas_call(
        functools.partial(_ffn_kernel, final_norm=final_norm),
        grid=(m // tm, d_ff // tf),
        in_specs=[
            pl.BlockSpec((tm, d), lambda i, j: (i, 0)),
            pl.BlockSpec((1, d), lambda i, j: (0, 0)),
            pl.BlockSpec((d, tf), lambda i, j: (0, j)),
            pl.BlockSpec((d, tf), lambda i, j: (0, j)),
            pl.BlockSpec((tf, d), lambda i, j: (j, 0)),
            pl.BlockSpec((1, d), lambda i, j: (0, 0)),
        ],
        out_specs=pl.BlockSpec((tm, d), lambda i, j: (i, 0)),
        out_shape=jax.ShapeDtypeStruct((m, d), F32),
        scratch_shapes=[pltpu.VMEM((tm, d), BF16)],
        compiler_params=pltpu.CompilerParams(
            dimension_semantics=("parallel", "arbitrary"),
            vmem_limit_bytes=56 * MIB),
        name="ffn",
    )(x, norm_w, w_gate, w_up, w_down, final_w)


def _inproj_kernel(x_ref, nw_ref, w_ref, o_ref, h_ref):
    @pl.when(pl.program_id(1) == 0)
    def _():
        h_ref[...] = _rmsnorm_rows(x_ref[...], nw_ref[...]).astype(BF16)

    p = jnp.dot(h_ref[...], w_ref[...], preferred_element_type=F32)
    for k in range(o_ref.shape[0]):
        o_ref[k] = p[:, k * LANES:(k + 1) * LANES]


def _in_proj(x, norm_w, w_in, *, tm=1024, tn=512):
    m, d = x.shape
    n = w_in.shape[1]
    assert m % tm == 0 and n % tn == 0 and tn % LANES == 0
    kb = tn // LANES
    return pl.pallas_call(
        _inproj_kernel,
        grid=(m // tm, n // tn),
        in_specs=[
            pl.BlockSpec((tm, d), lambda i, j: (i, 0)),
            pl.BlockSpec((1, d), lambda i, j: (0, 0)),
            pl.BlockSpec((d, tn), lambda i, j: (0, j)),
        ],
        out_specs=pl.BlockSpec((kb, tm, LANES), lambda i, j: (j, i, 0)),
        out_shape=jax.ShapeDtypeStruct((n // LANES, m, LANES), F32),
        scratch_shapes=[pltpu.VMEM((tm, d), BF16)],
        compiler_params=pltpu.CompilerParams(
            dimension_semantics=("parallel", "arbitrary"),
            vmem_limit_bytes=48 * MIB),
        name="in_proj",
    )(x, norm_w, w_in)


def _shift_rows(x, s, row):
    return jnp.where(row >= s, pltpu.roll(x, s, axis=0), 0.0)


def _mixer_kernel(q_ref, f_ref, i_ref, g_ref, cb_ref, cc_ref, ch_ref, lb_ref, nw_ref, cw_ref,
                  o_ref, *, layer, group):
    seq = q_ref.shape[1]
    n_chunks = seq // CHUNK

    logits = lb_ref[...]
    e = jnp.exp(logits - jnp.max(logits, axis=0, keepdims=True))
    lb = jnp.sum(e[:layer + 1], axis=0, keepdims=True) / jnp.sum(e, axis=0, keepdims=True)
    one_m_lb = 1.0 - lb
    nw = nw_ref[...]

    row = lax.broadcasted_iota(jnp.int32, (CHUNK, HEAD_DIM), 0)
    causal = (lax.broadcasted_iota(jnp.int32, (CHUNK, CHUNK), 0)
              >= lax.broadcasted_iota(jnp.int32, (CHUNK, CHUNK), 1))

    def chunk(c, st_t):
        rows = pl.ds(pl.multiple_of(c * CHUNK, CHUNK), CHUNK)
        q = q_ref[0, rows, :]
        fl = f_ref[0, rows, :]
        v = i_ref[0, rows, :]
        gate = g_ref[0, rows, :]

        t = jnp.exp(-jnp.abs(fl))
        r = 1.0 / (1.0 + t)
        pos = fl >= 0.0
        sig = jnp.where(pos, r, t * r)
        sig_neg = jnp.where(pos, t * r, r)
        log_f = jnp.log(lb + one_m_lb * sig)
        k = one_m_lb * sig_neg

        b = log_f
        s = 1
        while s < CHUNK:
            b = b + _shift_rows(b, s, row)
            s *= 2
        b_last = b[CHUNK - 1:CHUNK, :]
        q_dec = (q * jnp.exp(b)).astype(BF16)
        k_dec = (k * jnp.exp(-b)).astype(BF16)
        k_state = (k * jnp.exp(b_last - b)).astype(BF16)
        decay = jnp.exp(b_last)
        v16 = v.astype(BF16)

        scores = lax.dot_general(q_dec, k_dec, (((1,), (1,)), ((), ())),
                                 preferred_element_type=F32)
        scores = jnp.where(causal, scores, 0.0).astype(BF16)
        o = jnp.dot(scores, v16, preferred_element_type=F32)
        o = o + lax.dot_general(q_dec, st_t.astype(BF16), (((1,), (1,)), ((), ())),
                                preferred_element_type=F32)
        st_t = st_t * decay + jnp.dot(v.T.astype(BF16), k_state, preferred_element_type=F32)

        o = o * lax.rsqrt(jnp.mean(o * o, axis=-1, keepdims=True) + RMS_EPS) * nw
        o = o * (gate * (1.0 / (1.0 + jnp.exp(-gate))))
        o_ref[0, 0, rows, :] = o.astype(o_ref.dtype)
        return st_t

    def chunk_group(gi, st_t):
        for u in range(group):
            st_t = chunk(gi * group + u, st_t)
        return st_t

    lax.fori_loop(0, n_chunks // group, chunk_group, jnp.zeros((HEAD_DIM, HEAD_DIM), F32))

    cw = cw_ref[...]
    srow = lax.broadcasted_iota(jnp.int32, (seq, LANES), 0)
    u_in = cc_ref[0] * ch_ref[0]
    y = cw[CONV_K - 1:CONV_K, :] * u_in
    for tap in range(CONV_K - 1):
        y = y + cw[tap:tap + 1, :] * _shift_rows(u_in, CONV_K - 1 - tap, srow)
    o_ref[1, 0] = (cb_ref[0] * y).astype(o_ref.dtype)


def _mixer(proj, lb_logits, out_norm_w, conv_w, *, layer, batch, heads, group=8):
    n_blk, m, lanes = proj.shape
    assert lanes == LANES == HEAD_DIM
    seq = m // batch
    assert seq % (CHUNK * group) == 0

    def col(off):
        return pl.BlockSpec((1, seq, LANES), lambda b, h, off=off: (off + h, b, 0))

    return pl.pallas_call(
        functools.partial(_mixer_kernel, layer=layer, group=group),
        grid=(batch, heads),
        in_specs=[col(0), col(heads), col(2 * heads), col(3 * heads),
                  col(4 * heads), col(5 * heads), col(6 * heads),
                  pl.BlockSpec((lb_logits.shape[0], LANES), lambda b, h: (0, h)),
                  pl.BlockSpec((1, LANES), lambda b, h: (0, h)),
                  pl.BlockSpec((CONV_K, LANES), lambda b, h: (0, h))],
        out_specs=pl.BlockSpec((2, 1, seq, LANES), lambda b, h: (0, h, b, 0)),
        out_shape=jax.ShapeDtypeStruct((2, heads, m, LANES), BF16),
        compiler_params=pltpu.CompilerParams(
            dimension_semantics=("parallel", "parallel"),
            vmem_limit_bytes=40 * MIB),
        name="mixer",
    )(proj, proj, proj, proj, proj, proj, proj, lb_logits, out_norm_w, conv_w)


def _outproj_kernel(oc_ref, x_ref, w_ref, o_ref, lhs_ref):
    for k in range(oc_ref.shape[0]):
        lhs_ref[:, k * LANES:(k + 1) * LANES] = oc_ref[k]
    o_ref[...] = x_ref[...] + jnp.dot(lhs_ref[...], w_ref[...], preferred_element_type=F32)


def _out_proj(oc, x, w_out, *, tm=512):
    n_blk, m, _ = oc.shape
    d_mix, d = w_out.shape
    assert n_blk * LANES == d_mix and m % tm == 0
    return pl.pallas_call(
        _outproj_kernel,
        grid=(m // tm,),
        in_specs=[
            pl.BlockSpec((n_blk, tm, LANES), lambda i: (0, i, 0)),
            pl.BlockSpec((tm, d), lambda i: (i, 0)),
            pl.BlockSpec((d_mix, d), lambda i: (0, 0)),
        ],
        out_specs=pl.BlockSpec((tm, d), lambda i: (i, 0)),
        out_shape=jax.ShapeDtypeStruct((m, d), F32),
        scratch_shapes=[pltpu.VMEM((tm, d_mix), BF16)],
        compiler_params=pltpu.CompilerParams(
            dimension_semantics=("parallel",),
            vmem_limit_bytes=48 * MIB),
        name="out_proj",
    )(oc, x, w_out)


def kernel(x, ffn1_norm_w, ffn1_w_gate, ffn1_w_up, ffn1_w_down, mix_norm_w, w_in_mix,
           hgrn_lb_logits, hgrn_out_norm_w, conv_w, w_out_mix, ffn2_norm_w, ffn2_w_gate,
           ffn2_w_up, ffn2_w_down, final_norm_w):
    batch, seq, d = x.shape
    depth = ffn1_norm_w.shape[0]
    m = batch * seq
    heads = hgrn_out_norm_w.shape[1] // HEAD_DIM
    assert depth >= 1 and conv_w.shape[2] == heads * LANES
    final_w = final_norm_w.reshape(1, d)

    y = x.reshape(m, d)
    for l in range(depth):
        last = l == depth - 1
        y = _ffn(y, ffn1_norm_w[l].reshape(1, d), ffn1_w_gate[l].astype(BF16),
                 ffn1_w_up[l].astype(BF16), ffn1_w_down[l].astype(BF16), final_w,
                 final_norm=False)
        proj = _in_proj(y, mix_norm_w[l].reshape(1, d), w_in_mix[l].astype(BF16))
        oc = _mixer(proj, hgrn_lb_logits, hgrn_out_norm_w[l].reshape(1, -1), conv_w[l],
                    layer=l, batch=batch, heads=heads)
        y = _out_proj(oc.reshape(2 * heads, m, LANES), y, w_out_mix[l].astype(BF16))
        y = _ffn(y, ffn2_norm_w[l].reshape(1, d), ffn2_w_gate[l].astype(BF16),
                 ffn2_w_up[l].astype(BF16), ffn2_w_down[l].astype(BF16), final_w,
                 final_norm=last)
    return y.reshape(batch, seq, d)
```

```python
import functools

import jax
import jax.numpy as jnp
from jax import lax
from jax.experimental import pallas as pl
from jax.experimental.pallas import tpu as pltpu

LANES = 128
CHUNK = 64
HEAD_DIM = 128
CONV_K = 3
RMS_EPS = 1e-6
FFN_RESIDUAL_SCALE = 0.5
MIB = 1024 * 1024
ROW_CHUNK = 16
ROW_CHUNK_UNROLL = 8

F32 = jnp.float32
BF16 = jnp.bfloat16


def _rmsnorm_rows(x, w):
    return x * lax.rsqrt(jnp.mean(x * x, axis=-1, keepdims=True) + RMS_EPS) * w


def _for_row_chunks(n_rows, fn):
    def body(r, carry):
        fn(pl.ds(pl.multiple_of(r * ROW_CHUNK, ROW_CHUNK), ROW_CHUNK))
        return carry
    lax.fori_loop(0, n_rows // ROW_CHUNK, body, 0, unroll=ROW_CHUNK_UNROLL)


def _ffn_kernel(x_ref, nw_ref, wg_ref, wu_ref, wd_ref, fw_ref, o_ref, h_ref, r_ref, *,
                final_norm):
    j = pl.program_id(1)
    tm = x_ref.shape[0]

    @pl.when(j == 0)
    def _():
        def norm_rows(rows):
            h_ref[rows, :] = _rmsnorm_rows(x_ref[rows, :], nw_ref[...]).astype(BF16)
            o_ref[rows, :] = jnp.zeros((ROW_CHUNK, o_ref.shape[1]), F32)
        _for_row_chunks(tm, norm_rows)

    h = h_ref[...]
    g = jnp.dot(h, wg_ref[...], preferred_element_type=F32)
    u = jnp.dot(h, wu_ref[...], preferred_element_type=F32)
    a = (g * (1.0 / (1.0 + jnp.exp(-g)))) * u
    o_ref[...] += jnp.dot(a.astype(BF16), wd_ref[...], preferred_element_type=F32)

    @pl.when(j == pl.num_programs(1) - 1)
    def _():
        def residual(rows):
            return x_ref[rows, :] + FFN_RESIDUAL_SCALE * o_ref[rows, :]

        if final_norm:
            def row_stats(rows):
                y = residual(rows)
                ms = jnp.mean(y * y, axis=-1, keepdims=True)
                r_ref[rows, :] = jnp.broadcast_to(lax.rsqrt(ms + RMS_EPS), (ROW_CHUNK, LANES))
            _for_row_chunks(tm, row_stats)

        def finish_rows(rows):
            y = residual(rows)
            if final_norm:
                y = y * pltpu.repeat(r_ref[rows, :], y.shape[1] // LANES, axis=1) * fw_ref[...]
            o_ref[rows, :] = y
        _for_row_chunks(tm, finish_rows)


def _ffn(x, norm_w, w_gate, w_up, w_down, final_w, *, final_norm, tm=1024, tf=512):
    m, d = x.shape
    d_ff = w_gate.shape[1]
    assert m % tm == 0 and d_ff % tf == 0
    return pl.pallas_call(
        functools.partial(_ffn_kernel, final_norm=final_norm),
        grid=(m // tm, d_ff // tf),
        in_specs=[
            pl.BlockSpec((tm, d), lambda i, j: (i, 0)),
            pl.BlockSpec((1, d), lambda i, j: (0, 0)),
            pl.BlockSpec((d, tf), lambda i, j: (0, j)),
            pl.BlockSpec((d, tf), lambda i, j: (0, j)),
            pl.BlockSpec((tf, d), lambda i, j: (j, 0)),
            pl.BlockSpec((1, d), lambda i, j: (0, 0)),
        ],
        out_specs=pl.BlockSpec((tm, d), lambda i, j: (i, 0)),
        out_shape=jax.ShapeDtypeStruct((m, d), F32),
        scratch_shapes=[pltpu.VMEM((tm, d), BF16), pltpu.VMEM((tm, LANES), F32)],
        compiler_params=pltpu.CompilerParams(
            dimension_semantics=("parallel", "arbitrary"),
            vmem_limit_bytes=60 * MIB),
        name="ffn",
    )(x, norm_w, w_gate, w_up, w_down, final_w)


def _inproj_kernel(x_ref, nw_ref, w_ref, o_ref, h_ref):
    @pl.when(pl.program_id(1) == 0)
    def _():
        h_ref[...] = _rmsnorm_rows(x_ref[...], nw_ref[...]).astype(BF16)

    p = jnp.dot(h_ref[...], w_ref[...], preferred_element_type=F32)
    for k in range(o_ref.shape[0]):
        o_ref[k] = p[:, k * LANES:(k + 1) * LANES]


def _in_proj(x, norm_w, w_in, *, tm=1024, tn=512):
    m, d = x.shape
    n = w_in.shape[1]
    assert m % tm == 0 and n % tn == 0 and tn % LANES == 0
    kb = tn // LANES
    return pl.pallas_call(
        _inproj_kernel,
        grid=(m // tm, n // tn),
        in_specs=[
            pl.BlockSpec((tm, d), lambda i, j: (i, 0)),
            pl.BlockSpec((1, d), lambda i, j: (0, 0)),
            pl.BlockSpec((d, tn), lambda i, j: (0, j)),
        ],
        out_specs=pl.BlockSpec((kb, tm, LANES), lambda i, j: (j, i, 0)),
        out_shape=jax.ShapeDtypeStruct((n // LANES, m, LANES), F32),
        scratch_shapes=[pltpu.VMEM((tm, d), BF16)],
        compiler_params=pltpu.CompilerParams(
            dimension_semantics=("parallel", "arbitrary"),
            vmem_limit_bytes=48 * MIB),
        name="in_proj",
    )(x, norm_w, w_in)


def _shift_rows(x, s, row):
    return jnp.where(row >= s, pltpu.roll(x, s, axis=0), 0.0)


def _mixer_kernel(q_ref, f_ref, i_ref, g_ref, cb_ref, cc_ref, ch_ref, lb_ref, nw_ref, cw_ref,
                  o_ref, *, layer, group):
    seq = q_ref.shape[1]
    n_chunks = seq // CHUNK

    logits = lb_ref[...]
    e = jnp.exp(logits - jnp.max(logits, axis=0, keepdims=True))
    lb = jnp.sum(e[:layer + 1], axis=0, keepdims=True) / jnp.sum(e, axis=0, keepdims=True)
    one_m_lb = 1.0 - lb
    nw = nw_ref[...]

    row = lax.broadcasted_iota(jnp.int32, (CHUNK, HEAD_DIM), 0)
    causal = (lax.broadcasted_iota(jnp.int32, (CHUNK, CHUNK), 0)
              >= lax.broadcasted_iota(jnp.int32, (CHUNK, CHUNK), 1))

    def chunk(c, st_t):
        rows = pl.ds(pl.multiple_of(c * CHUNK, CHUNK), CHUNK)
        q = q_ref[0, rows, :]
        fl = f_ref[0, rows, :]
        v = i_ref[0, rows, :]
        gate = g_ref[0, rows, :]

        t = jnp.exp(-jnp.abs(fl))
        r = 1.0 / (1.0 + t)
        pos = fl >= 0.0
        sig = jnp.where(pos, r, t * r)
        sig_neg = jnp.where(pos, t * r, r)
        log_f = jnp.log(lb + one_m_lb * sig)
        k = one_m_lb * sig_neg

        b = log_f
        s = 1
        while s < CHUNK:
            b = b + _shift_rows(b, s, row)
            s *= 2
        b_last = b[CHUNK - 1:CHUNK, :]
        decay = jnp.exp(b_last)
        q_dec = (q * jnp.exp(b)).astype(BF16)
        k_dec32 = k * jnp.exp(-b)
        k_dec = k_dec32.astype(BF16)
        k_state = (k_dec32 * decay).astype(BF16)
        v16 = v.astype(BF16)

        scores = lax.dot_general(q_dec, k_dec, (((1,), (1,)), ((), ())),
                                 preferred_element_type=F32)
        scores = jnp.where(causal, scores, 0.0).astype(BF16)
        o = jnp.dot(scores, v16, preferred_element_type=F32)
        o = o + lax.dot_general(q_dec, st_t.astype(BF16), (((1,), (1,)), ((), ())),
                                preferred_element_type=F32)
        st_t = st_t * decay + jnp.dot(v.T.astype(BF16), k_state, preferred_element_type=F32)

        o = o * lax.rsqrt(jnp.mean(o * o, axis=-1, keepdims=True) + RMS_EPS) * nw
        o = o * (gate * (1.0 / (1.0 + jnp.exp(-gate))))
        o_ref[0, 0, rows, :] = o.astype(o_ref.dtype)
        return st_t

    def chunk_group(gi, st_t):
        for u in range(group):
            st_t = chunk(gi * group + u, st_t)
        return st_t

    lax.fori_loop(0, n_chunks // group, chunk_group, jnp.zeros((HEAD_DIM, HEAD_DIM), F32))

    cw = cw_ref[...]
    srow = lax.broadcasted_iota(jnp.int32, (seq, LANES), 0)
    u_in = cc_ref[0] * ch_ref[0]
    y = cw[CONV_K - 1:CONV_K, :] * u_in
    for tap in range(CONV_K - 1):
        y = y + cw[tap:tap + 1, :] * _shift_rows(u_in, CONV_K - 1 - tap, srow)
    o_ref[1, 0] = (cb_ref[0] * y).astype(o_ref.dtype)


def _mixer(proj, lb_logits, out_norm_w, conv_w, *, layer, batch, heads, group=8):
    n_blk, m, lanes = proj.shape
    assert lanes == LANES == HEAD_DIM
    seq = m // batch
    assert seq % (CHUNK * group) == 0

    def col(off):
        return pl.BlockSpec((1, seq, LANES), lambda b, h, off=off: (off + h, b, 0))

    return pl.pallas_call(
        functools.partial(_mixer_kernel, layer=layer, group=group),
        grid=(batch, heads),
        in_specs=[col(0), col(heads), col(2 * heads), col(3 * heads),
                  col(4 * heads), col(5 * heads), col(6 * heads),
                  pl.BlockSpec((lb_logits.shape[0], LANES), lambda b, h: (0, h)),
                  pl.BlockSpec((1, LANES), lambda b, h: (0, h)),
                  pl.BlockSpec((CONV_K, LANES), lambda b, h: (0, h))],
        out_specs=pl.BlockSpec((2, 1, seq, LANES), lambda b, h: (0, h, b, 0)),
        out_shape=jax.ShapeDtypeStruct((2, heads, m, LANES), BF16),
        compiler_params=pltpu.CompilerParams(
            dimension_semantics=("parallel", "parallel"),
            vmem_limit_bytes=40 * MIB),
        name="mixer",
    )(proj, proj, proj, proj, proj, proj, proj, lb_logits, out_norm_w, conv_w)


def _outproj_kernel(oc_ref, x_ref, w_ref, o_ref, lhs_ref):
    for k in range(oc_ref.shape[0]):
        lhs_ref[:, k * LANES:(k + 1) * LANES] = oc_ref[k]
    o_ref[...] = x_ref[...] + jnp.dot(lhs_ref[...], w_ref[...], preferred_element_type=F32)


def _out_proj(oc, x, w_out, *, tm=512):
    n_blk, m, _ = oc.shape
    d_mix, d = w_out.shape
    assert n_blk * LANES == d_mix and m % tm == 0
    return pl.pallas_call(
        _outproj_kernel,
        grid=(m // tm,),
        in_specs=[
            pl.BlockSpec((n_blk, tm, LANES), lambda i: (0, i, 0)),
            pl.BlockSpec((tm, d), lambda i: (i, 0)),
            pl.BlockSpec((d_mix, d), lambda i: (0, 0)),
        ],
        out_specs=pl.BlockSpec((tm, d), lambda i: (i, 0)),
        out_shape=jax.ShapeDtypeStruct((m, d), F32),
        scratch_shapes=[pltpu.VMEM((tm, d_mix), BF16)],
        compiler_params=pltpu.CompilerParams(
            dimension_semantics=("parallel",),
            vmem_limit_bytes=48 * MIB),
        name="out_proj",
    )(oc, x, w_out)


def kernel(x, ffn1_norm_w, ffn1_w_gate, ffn1_w_up, ffn1_w_down, mix_norm_w, w_in_mix,
           hgrn_lb_logits, hgrn_out_norm_w, conv_w, w_out_mix, ffn2_norm_w, ffn2_w_gate,
           ffn2_w_up, ffn2_w_down, final_norm_w):
    batch, seq, d = x.shape
    depth = ffn1_norm_w.shape[0]
    m = batch * seq
    heads = hgrn_out_norm_w.shape[1] // HEAD_DIM
    assert depth >= 1 and conv_w.shape[2] == heads * LANES
    final_w = final_norm_w.reshape(1, d)

    y = x.reshape(m, d)
    for l in range(depth):
        last = l == depth - 1
        y = _ffn(y, ffn1_norm_w[l].reshape(1, d), ffn1_w_gate[l].astype(BF16),
                 ffn1_w_up[l].astype(BF16), ffn1_w_down[l].astype(BF16), final_w,
                 final_norm=False)
        proj = _in_proj(y, mix_norm_w[l].reshape(1, d), w_in_mix[l].astype(BF16))
        oc = _mixer(proj, hgrn_lb_logits, hgrn_out_norm_w[l].reshape(1, -1), conv_w[l],
                    layer=l, batch=batch, heads=heads)
        y = _out_proj(oc.reshape(2 * heads, m, LANES), y, w_out_mix[l].astype(BF16))
        y = _ffn(y, ffn2_norm_w[l].reshape(1, d), ffn2_w_gate[l].astype(BF16),
                 ffn2_w_up[l].astype(BF16), ffn2_w_down[l].astype(BF16), final_w,
                 final_norm=last)
    return y.reshape(batch, seq, d)
```

```python
import functools

import jax
import jax.numpy as jnp
from jax import lax
from jax.experimental import pallas as pl
from jax.experimental.pallas import tpu as pltpu

LANES = 128
SUBLANES = 8
CHUNK = 64
HEAD_DIM = 128
PAIR = 2
N_SECTIONS = 7
CONV_K = 3
RMS_EPS = 1e-6
FFN_RESIDUAL_SCALE = 0.5
MIB = 1024 * 1024
ROW_CHUNK = 16
ROW_CHUNK_UNROLL = 8

F32 = jnp.float32
BF16 = jnp.bfloat16


def _rmsnorm_rows(x, w):
    return x * lax.rsqrt(jnp.mean(x * x, axis=-1, keepdims=True) + RMS_EPS) * w


def _for_row_chunks(n_rows, fn):
    def body(r, carry):
        fn(pl.ds(pl.multiple_of(r * ROW_CHUNK, ROW_CHUNK), ROW_CHUNK))
        return carry
    lax.fori_loop(0, n_rows // ROW_CHUNK, body, 0, unroll=ROW_CHUNK_UNROLL)


def _sigmoid(x):
    return 1.0 / (1.0 + jnp.exp(-x))


def _ffn_kernel(x_ref, nw_ref, wg_ref, wu_ref, wd_ref, fw_ref, o_ref, h_ref, r_ref, *,
                final_norm):
    j = pl.program_id(1)
    tm, d = x_ref.shape

    @pl.when(j == 0)
    def _():
        def norm_rows(rows):
            h_ref[rows, :] = _rmsnorm_rows(x_ref[rows, :], nw_ref[...]).astype(BF16)
            o_ref[rows, :] = jnp.zeros((ROW_CHUNK, d), F32)
        _for_row_chunks(tm, norm_rows)

    h = h_ref[...]
    g = jnp.dot(h, wg_ref[...], preferred_element_type=F32)
    u = jnp.dot(h, wu_ref[...], preferred_element_type=F32)
    a = (g * _sigmoid(g)) * u
    o_ref[...] += jnp.dot(a.astype(BF16), wd_ref[...], preferred_element_type=F32)

    @pl.when(j == pl.num_programs(1) - 1)
    def _():
        def residual(rows):
            return x_ref[rows, :] + FFN_RESIDUAL_SCALE * o_ref[rows, :]

        if final_norm:
            def row_stats(rows):
                y = residual(rows)
                ms = jnp.mean(y * y, axis=-1, keepdims=True)
                r_ref[rows, :] = jnp.broadcast_to(lax.rsqrt(ms + RMS_EPS), (ROW_CHUNK, LANES))
            _for_row_chunks(tm, row_stats)

        def finish_rows(rows):
            y = residual(rows)
            if final_norm:
                r = r_ref[rows, :]
                y = y * jnp.concatenate([r] * (d // LANES), axis=1) * fw_ref[...]
            o_ref[rows, :] = y
        _for_row_chunks(tm, finish_rows)


def _ffn(x, norm_w, w_gate, w_up, w_down, final_w, *, final_norm, tm=1024, tf=512):
    m, d = x.shape
    d_ff = w_gate.shape[1]
    assert m % tm == 0 and d_ff % tf == 0
    return pl.pallas_call(
        functools.partial(_ffn_kernel, final_norm=final_norm),
        grid=(m // tm, d_ff // tf),
        in_specs=[
            pl.BlockSpec((tm, d), lambda i, j: (i, 0)),
            pl.BlockSpec((1, d), lambda i, j: (0, 0)),
            pl.BlockSpec((d, tf), lambda i, j: (0, j)),
            pl.BlockSpec((d, tf), lambda i, j: (0, j)),
            pl.BlockSpec((tf, d), lambda i, j: (j, 0)),
            pl.BlockSpec((1, d), lambda i, j: (0, 0)),
        ],
        out_specs=pl.BlockSpec((tm, d), lambda i, j: (i, 0)),
        out_shape=jax.ShapeDtypeStruct((m, d), F32),
        scratch_shapes=[pltpu.VMEM((tm, d), BF16), pltpu.VMEM((tm, LANES), F32)],
        compiler_params=pltpu.CompilerParams(
            dimension_semantics=("parallel", "arbitrary"),
            vmem_limit_bytes=60 * MIB),
        name="ffn",
    )(x, norm_w, w_gate, w_up, w_down, final_w)


def _inproj_kernel(x_ref, nw_ref, w_ref, o_ref, h_ref):
    @pl.when(pl.program_id(1) == 0)
    def _():
        h_ref[...] = _rmsnorm_rows(x_ref[...], nw_ref[...]).astype(BF16)

    p = jnp.dot(h_ref[...], w_ref[...], preferred_element_type=F32)
    for k in range(o_ref.shape[0]):
        o_ref[k] = p[:, k * LANES:(k + 1) * LANES]


def _in_proj(x, norm_w, w_in, *, tm=1024, tn=1024):
    m, d = x.shape
    n = w_in.shape[1]
    assert m % tm == 0 and n % tn == 0 and tn % LANES == 0
    kb = tn // LANES
    return pl.pallas_call(
        _inproj_kernel,
        grid=(m // tm, n // tn),
        in_specs=[
            pl.BlockSpec((tm, d), lambda i, j: (i, 0)),
            pl.BlockSpec((1, d), lambda i, j: (0, 0)),
            pl.BlockSpec((d, tn), lambda i, j: (0, j)),
        ],
        out_specs=pl.BlockSpec((kb, tm, LANES), lambda i, j: (j, i, 0)),
        out_shape=jax.ShapeDtypeStruct((n // LANES, m, LANES), F32),
        scratch_shapes=[pltpu.VMEM((tm, d), BF16)],
        compiler_params=pltpu.CompilerParams(
            dimension_semantics=("parallel", "arbitrary"),
            vmem_limit_bytes=52 * MIB),
        name="in_proj",
    )(x, norm_w, w_in)


def _shift_rows(x, s, row):
    return jnp.where(row >= s, pltpu.roll(x, s, axis=0), 0.0)


def _mixer_kernel(q_ref, f_ref, i_ref, g_ref, cb_ref, cc_ref, ch_ref, lb_ref, nw_ref, cw_ref,
                  o_ref, st_ref, halo_ref, *, layer, group):
    seq = q_ref.shape[1]
    n_chunks = seq // CHUNK

    logits = lb_ref[...]
    e = jnp.exp(logits - jnp.max(logits, axis=0, keepdims=True))
    lb_all = jnp.sum(e[:layer + 1], axis=0, keepdims=True) / jnp.sum(e, axis=0, keepdims=True)
    nw_all = nw_ref[...]

    slab = group * CHUNK
    causal = (lax.broadcasted_iota(jnp.int32, (CHUNK, CHUNK), 0)
              >= lax.broadcasted_iota(jnp.int32, (CHUNK, CHUNK), 1))
    srow_i = lax.broadcasted_iota(jnp.int32, (slab, slab), 0)
    scol_i = lax.broadcasted_iota(jnp.int32, (slab, slab), 1)
    tril16 = ((srow_i >= scol_i) & (srow_i // CHUNK == scol_i // CHUNK)).astype(BF16)
    top_row = lax.broadcasted_iota(jnp.int32, (SUBLANES, LANES), 0)
    heads_in_step = range(PAIR)
    chunks = [slice(c * CHUNK, (c + 1) * CHUNK) for c in range(group)]

    def slab_step(gi, carry):
        rows = pl.ds(pl.multiple_of(gi * slab, slab), slab)
        lbs = [lb_all[:, hh * HEAD_DIM:(hh + 1) * HEAD_DIM] for hh in heads_in_step]

        hl, ks = [], []
        for hh in heads_in_step:
            fl = f_ref[hh, rows, :]
            f = lbs[hh] + (1.0 - lbs[hh]) * _sigmoid(fl)
            log_f = jnp.log(f)
            ks.append(1.0 - f)
            hi = log_f.astype(BF16)
            lo = (log_f - hi.astype(F32)).astype(BF16)
            hl.append(jnp.concatenate([hi, lo], axis=1))

        bs = []
        for hh in heads_in_step:
            bb = jnp.dot(tril16, hl[hh], preferred_element_type=F32)
            bs.append(bb[:, :HEAD_DIM] + bb[:, HEAD_DIM:])

        q_decs, k_decs, k_states, decays, v16s = [], [], [], [], []
        for hh in heads_in_step:
            b = bs[hh]
            eb = jnp.exp(b)
            q_decs.append((q_ref[hh, rows, :] * eb).astype(BF16))
            k_dec32 = ks[hh] * (1.0 / eb)
            k_decs.append(k_dec32.astype(BF16))
            dec = [jnp.exp(b[ch.stop - 1:ch.stop, :]) for ch in chunks]
            decays.append(dec)
            k_states.append([(k_dec32[ch] * d).astype(BF16) for ch, d in zip(chunks, dec)])
            v16s.append(i_ref[hh, rows, :].astype(BF16))

        scores, updates = [], []
        for hh in heads_in_step:
            v32 = i_ref[hh, rows, :]
            scores.append([lax.dot_general(q_decs[hh][ch], k_decs[hh][ch],
                                           (((1,), (1,)), ((), ())),
                                           preferred_element_type=F32) for ch in chunks])
            updates.append([jnp.dot(v32[ch].T.astype(BF16), k_states[hh][c],
                                    preferred_element_type=F32)
                            for c, ch in enumerate(chunks)])

        o_parts = []
        for hh in heads_in_step:
            o_intra = [jnp.dot(jnp.where(causal, scores[hh][c], 0.0).astype(BF16),
                               v16s[hh][ch], preferred_element_type=F32)
                       for c, ch in enumerate(chunks)]
            st_t = st_ref[hh]
            o_inter = []
            for c, ch in enumerate(chunks):
                o_inter.append(lax.dot_general(q_decs[hh][ch], st_t.astype(BF16),
                                               (((1,), (1,)), ((), ())),
                                               preferred_element_type=F32))
                st_t = st_t * decays[hh][c] + updates[hh][c]
            st_ref[hh] = st_t
            o_parts.append(jnp.concatenate([a + b for a, b in zip(o_intra, o_inter)], axis=0))

        for hh in heads_in_step:
            o = o_parts[hh]
            gate = g_ref[hh, rows, :]
            nw = nw_all[:, hh * HEAD_DIM:(hh + 1) * HEAD_DIM]
            o = o * lax.rsqrt(jnp.mean(o * o, axis=-1, keepdims=True) + RMS_EPS) * nw
            o = o * (gate * _sigmoid(gate))
            o_ref[0, hh, rows, :] = o.astype(o_ref.dtype)

        for hh in heads_in_step:
            cw = cw_ref[:, hh * LANES:(hh + 1) * LANES]
            u_in = cc_ref[hh, rows, :] * ch_ref[hh, rows, :]
            halo = halo_ref[hh]
            y = cw[CONV_K - 1:CONV_K, :] * u_in
            for tap in range(CONV_K - 1):
                shift = CONV_K - 1 - tap
                shifted = pltpu.roll(u_in, shift, axis=0)
                top = shifted[:SUBLANES]
                for r0 in range(shift):
                    hrow = SUBLANES - shift + r0
                    top = jnp.where(top_row == r0, halo[hrow:hrow + 1, :], top)
                shifted = jnp.concatenate([top, shifted[SUBLANES:]], axis=0)
                y = y + cw[tap:tap + 1, :] * shifted
            halo_ref[hh] = u_in[slab - SUBLANES:, :]
            o_ref[1, hh, rows, :] = (cb_ref[hh, rows, :] * y).astype(o_ref.dtype)
        return carry

    st_ref[...] = jnp.zeros_like(st_ref)
    halo_ref[...] = jnp.zeros_like(halo_ref)
    lax.fori_loop(0, n_chunks // group, slab_step, 0)


def _mixer(proj, lb_logits, out_norm_w, conv_w, *, layer, batch, heads, group=4):
    n_blk, m, lanes = proj.shape
    assert lanes == LANES == HEAD_DIM and n_blk == N_SECTIONS * heads and heads % PAIR == 0
    seq = m // batch
    n_pairs = heads // PAIR
    pw = PAIR * LANES
    assert seq % (CHUNK * group) == 0

    def col(section):
        return pl.BlockSpec((PAIR, seq, LANES),
                            lambda b, p, s=section: (s * n_pairs + p, b, 0))

    def vec(rows):
        return pl.BlockSpec((rows, pw), lambda b, p: (0, p))

    return pl.pallas_call(
        functools.partial(_mixer_kernel, layer=layer, group=group),
        grid=(batch, n_pairs),
        in_specs=[col(s) for s in range(N_SECTIONS)]
                 + [vec(lb_logits.shape[0]), vec(1), vec(CONV_K)],
        out_specs=pl.BlockSpec((2, PAIR, seq, LANES), lambda b, p: (0, p, b, 0)),
        out_shape=jax.ShapeDtypeStruct((2, heads, m, LANES), BF16),
        scratch_shapes=[pltpu.VMEM((PAIR, HEAD_DIM, HEAD_DIM), F32),
                        pltpu.VMEM((PAIR, SUBLANES, LANES), F32)],
        compiler_params=pltpu.CompilerParams(
            dimension_semantics=("parallel", "parallel"),
            vmem_limit_bytes=52 * MIB),
        name="mixer",
    )(*([proj] * N_SECTIONS), lb_logits, out_norm_w, conv_w)


def _outproj_kernel(oc_ref, x_ref, w_ref, o_ref, lhs_ref):
    for k in range(oc_ref.shape[0]):
        lhs_ref[:, k * LANES:(k + 1) * LANES] = oc_ref[k]
    o_ref[...] = x_ref[...] + jnp.dot(lhs_ref[...], w_ref[...], preferred_element_type=F32)


def _out_proj(oc, x, w_out, *, tm=512):
    n_blk, m, _ = oc.shape
    d_mix, d = w_out.shape
    assert n_blk * LANES == d_mix and m % tm == 0
    return pl.pallas_call(
        _outproj_kernel,
        grid=(m // tm,),
        in_specs=[
            pl.BlockSpec((n_blk, tm, LANES), lambda i: (0, i, 0)),
            pl.BlockSpec((tm, d), lambda i: (i, 0)),
            pl.BlockSpec((d_mix, d), lambda i: (0, 0)),
        ],
        out_specs=pl.BlockSpec((tm, d), lambda i: (i, 0)),
        out_shape=jax.ShapeDtypeStruct((m, d), F32),
        scratch_shapes=[pltpu.VMEM((tm, d_mix), BF16)],
        compiler_params=pltpu.CompilerParams(
            dimension_semantics=("parallel",),
            vmem_limit_bytes=48 * MIB),
        name="out_proj",
    )(oc, x, w_out)


def kernel(x, ffn1_norm_w, ffn1_w_gate, ffn1_w_up, ffn1_w_down, mix_norm_w, w_in_mix,
           hgrn_lb_logits, hgrn_out_norm_w, conv_w, w_out_mix, ffn2_norm_w, ffn2_w_gate,
           ffn2_w_up, ffn2_w_down, final_norm_w):
    batch, seq, d = x.shape
    depth = ffn1_norm_w.shape[0]
    m = batch * seq
    heads = hgrn_out_norm_w.shape[1] // HEAD_DIM
    assert depth >= 1 and conv_w.shape[2] == heads * LANES
    final_w = final_norm_w.reshape(1, d)

    y = x.reshape(m, d)
    for l in range(depth):
        last = l == depth - 1
        y = _ffn(y, ffn1_norm_w[l].reshape(1, d), ffn1_w_gate[l].astype(BF16),
                 ffn1_w_up[l].astype(BF16), ffn1_w_down[l].astype(BF16), final_w,
                 final_norm=False)
        proj = _in_proj(y, mix_norm_w[l].reshape(1, d), w_in_mix[l].astype(BF16))
        oc = _mixer(proj, hgrn_lb_logits, hgrn_out_norm_w[l].reshape(1, -1), conv_w[l],
                    layer=l, batch=batch, heads=heads)
        y = _out_proj(oc.reshape(2 * heads, m, LANES), y, w_out_mix[l].astype(BF16))
        y = _ffn(y, ffn2_norm_w[l].reshape(1, d), ffn2_w_gate[l].astype(BF16),
                 ffn2_w_up[l].astype(BF16), ffn2_w_down[l].astype(BF16), final_w,
                 final_norm=last)
    return y.reshape(batch, seq, d)
```

```python
import functools
import math

import jax
import jax.numpy as jnp
from jax import lax
from jax.experimental import pallas as pl
from jax.experimental.pallas import tpu as pltpu

LANES = 128
SUBLANES = 8
CHUNK = 64
HEAD_DIM = 128
PAIR = 2
N_SECTIONS = 7
CONV_K = 3
RMS_EPS = 1e-6
FFN_RESIDUAL_SCALE = 0.5
MIB = 1024 * 1024
BF16_SUBLANES = 16
V7X_VMEM_BYTES = 64 * MIB
VMEM_RESERVED_BYTES = 4 * MIB
VMEM_COMPILER_TEMP_BYTES = 8 * MIB
ROW_CHUNK = 16
ROW_CHUNK_UNROLL = 8

F32 = jnp.float32
BF16 = jnp.bfloat16


def _rmsnorm_rows(x, w):
    return x * lax.rsqrt(jnp.mean(x * x, axis=-1, keepdims=True) + RMS_EPS) * w


def _for_row_chunks(n_rows, fn):
    def body(r, carry):
        fn(pl.ds(pl.multiple_of(r * ROW_CHUNK, ROW_CHUNK), ROW_CHUNK))
        return carry
    lax.fori_loop(0, n_rows // ROW_CHUNK, body, 0, unroll=ROW_CHUNK_UNROLL)


def _sigmoid(x):
    return 1.0 / (1.0 + jnp.exp(-x))


def _nbytes(shape, dtype):
    return math.prod(shape) * jnp.dtype(dtype).itemsize


def _vmem_limit(pipelined, resident):
    estimate = (2 * sum(_nbytes(s, t) for s, t in pipelined)
                + sum(_nbytes(s, t) for s, t in resident) + VMEM_COMPILER_TEMP_BYTES)
    return min(estimate, V7X_VMEM_BYTES - VMEM_RESERVED_BYTES)


def _cast_rows_per_block(rows, n_steps):
    rpb = BF16_SUBLANES
    while rows % rpb or rows // rpb > n_steps:
        rpb += BF16_SUBLANES
        assert rpb <= rows, "no row blocking of this weight fits the grid"
    return rpb


def _cast_specs(weights, grid):
    n_inner = math.prod(grid[1:])
    specs, shapes = [], []
    for w in weights:
        rows, cols = w.shape
        rpb = _cast_rows_per_block(rows, math.prod(grid))
        last = rows // rpb - 1

        def index_map(i, *inner, last=last):
            step = i * n_inner + (inner[0] if inner else 0)
            return (jnp.minimum(step, last), 0)

        specs.append(pl.BlockSpec((rpb, cols), index_map))
        shapes.append(jax.ShapeDtypeStruct((rows, cols), BF16))
    return specs, shapes


def _run_casts(src_refs, dst_refs):
    for src, dst in zip(src_refs, dst_refs):
        dst[...] = src[...].astype(dst.dtype)


def _ffn_kernel(x_ref, nw_ref, wg_ref, wu_ref, wd_ref, fw_ref, *rest, final_norm, n_cast):
    cast_src, (o_ref, *cast_dst), (h_ref, r_ref) = (
        rest[:n_cast], rest[n_cast:2 * n_cast + 1], rest[2 * n_cast + 1:])
    j = pl.program_id(1)
    tm, d = x_ref.shape

    @pl.when(j == 0)
    def _():
        def norm_rows(rows):
            h_ref[rows, :] = _rmsnorm_rows(x_ref[rows, :], nw_ref[...]).astype(BF16)
            o_ref[rows, :] = jnp.zeros((ROW_CHUNK, d), F32)
        _for_row_chunks(tm, norm_rows)

    h = h_ref[...]
    g = jnp.dot(h, wg_ref[...], preferred_element_type=F32)
    u = jnp.dot(h, wu_ref[...], preferred_element_type=F32)
    _run_casts(cast_src, cast_dst)
    a = (g * _sigmoid(g)) * u
    o_ref[...] += jnp.dot(a.astype(BF16), wd_ref[...], preferred_element_type=F32)

    @pl.when(j == pl.num_programs(1) - 1)
    def _():
        def residual(rows):
            return x_ref[rows, :] + FFN_RESIDUAL_SCALE * o_ref[rows, :]

        if final_norm:
            def row_stats(rows):
                y = residual(rows)
                ms = jnp.mean(y * y, axis=-1, keepdims=True)
                r_ref[rows, :] = jnp.broadcast_to(lax.rsqrt(ms + RMS_EPS), (ROW_CHUNK, LANES))
            _for_row_chunks(tm, row_stats)

        def finish_rows(rows):
            y = residual(rows)
            if final_norm:
                r = r_ref[rows, :]
                y = y * jnp.concatenate([r] * (d // LANES), axis=1) * fw_ref[...]
            o_ref[rows, :] = y
        _for_row_chunks(tm, finish_rows)


def _ffn(x, norm_w, w_gate, w_up, w_down, final_w, *, final_norm, cast_weights=(),
         tm=1024, tf=512):
    m, d = x.shape
    d_ff = w_gate.shape[1]
    assert m % tm == 0 and d_ff % tf == 0
    grid = (m // tm, d_ff // tf)
    cast_specs, cast_shapes = _cast_specs(cast_weights, grid)
    windows = ([((tm, d), F32)] * 2 + [((d, tf), BF16)] * 3
               + [(s.block_shape, F32) for s in cast_specs]
               + [(s.block_shape, BF16) for s in cast_specs])
    scratch = [((tm, d), BF16), ((tm, LANES), F32)]
    return pl.pallas_call(
        functools.partial(_ffn_kernel, final_norm=final_norm, n_cast=len(cast_specs)),
        grid=grid,
        in_specs=[
            pl.BlockSpec((tm, d), lambda i, j: (i, 0)),
            pl.BlockSpec((1, d), lambda i, j: (0, 0)),
            pl.BlockSpec((d, tf), lambda i, j: (0, j)),
            pl.BlockSpec((d, tf), lambda i, j: (0, j)),
            pl.BlockSpec((tf, d), lambda i, j: (j, 0)),
            pl.BlockSpec((1, d), lambda i, j: (0, 0)),
        ] + cast_specs,
        out_specs=[pl.BlockSpec((tm, d), lambda i, j: (i, 0))] + cast_specs,
        out_shape=[jax.ShapeDtypeStruct((m, d), F32)] + cast_shapes,
        scratch_shapes=[pltpu.VMEM(s, t) for s, t in scratch],
        compiler_params=pltpu.CompilerParams(
            dimension_semantics=("arbitrary", "arbitrary"),
            vmem_limit_bytes=_vmem_limit(windows, scratch)),
        name="ffn",
    )(x, norm_w, w_gate, w_up, w_down, final_w, *cast_weights)


def _inproj_kernel(x_ref, nw_ref, w_ref, *rest, n_cast):
    cast_src, (o_ref, *cast_dst), (h_ref,) = (
        rest[:n_cast], rest[n_cast:2 * n_cast + 1], rest[2 * n_cast + 1:])

    @pl.when(pl.program_id(1) == 0)
    def _():
        h_ref[...] = _rmsnorm_rows(x_ref[...], nw_ref[...]).astype(BF16)

    p = jnp.dot(h_ref[...], w_ref[...], preferred_element_type=F32)
    _run_casts(cast_src, cast_dst)
    for k in range(o_ref.shape[0]):
        o_ref[k] = p[:, k * LANES:(k + 1) * LANES]


def _in_proj(x, norm_w, w_in, *, cast_weights=(), tm=1024, tn=1024):
    m, d = x.shape
    n = w_in.shape[1]
    assert m % tm == 0 and n % tn == 0 and tn % LANES == 0
    kb = tn // LANES
    grid = (m // tm, n // tn)
    cast_specs, cast_shapes = _cast_specs(cast_weights, grid)
    windows = ([((tm, d), F32), ((d, tn), BF16), ((kb, tm, LANES), F32)]
               + [(s.block_shape, F32) for s in cast_specs]
               + [(s.block_shape, BF16) for s in cast_specs])
    scratch = [((tm, d), BF16)]
    return pl.pallas_call(
        functools.partial(_inproj_kernel, n_cast=len(cast_specs)),
        grid=grid,
        in_specs=[
            pl.BlockSpec((tm, d), lambda i, j: (i, 0)),
            pl.BlockSpec((1, d), lambda i, j: (0, 0)),
            pl.BlockSpec((d, tn), lambda i, j: (0, j)),
        ] + cast_specs,
        out_specs=[pl.BlockSpec((kb, tm, LANES), lambda i, j: (j, i, 0))] + cast_specs,
        out_shape=[jax.ShapeDtypeStruct((n // LANES, m, LANES), F32)] + cast_shapes,
        scratch_shapes=[pltpu.VMEM(s, t) for s, t in scratch],
        compiler_params=pltpu.CompilerParams(
            dimension_semantics=("arbitrary", "arbitrary"),
            vmem_limit_bytes=_vmem_limit(windows, scratch)),
        name="in_proj",
    )(x, norm_w, w_in, *cast_weights)


def _mixer_kernel(q_ref, f_ref, i_ref, g_ref, cb_ref, cc_ref, ch_ref, lb_ref, nw_ref, cw_ref,
                  o_ref, st_ref, halo_ref, *, layer, group):
    seq = q_ref.shape[1]
    n_chunks = seq // CHUNK

    logits = lb_ref[...]
    e = jnp.exp(logits - jnp.max(logits, axis=0, keepdims=True))
    lb_all = jnp.sum(e[:layer + 1], axis=0, keepdims=True) / jnp.sum(e, axis=0, keepdims=True)
    nw_all = nw_ref[...]

    slab = group * CHUNK
    causal = (lax.broadcasted_iota(jnp.int32, (CHUNK, CHUNK), 0)
              >= lax.broadcasted_iota(jnp.int32, (CHUNK, CHUNK), 1))
    srow_i = lax.broadcasted_iota(jnp.int32, (slab, slab), 0)
    scol_i = lax.broadcasted_iota(jnp.int32, (slab, slab), 1)
    tril16 = ((srow_i >= scol_i) & (srow_i // CHUNK == scol_i // CHUNK)).astype(BF16)
    top_row = lax.broadcasted_iota(jnp.int32, (SUBLANES, LANES), 0)
    heads_in_step = range(PAIR)
    chunks = [slice(c * CHUNK, (c + 1) * CHUNK) for c in range(group)]

    def slab_step(gi, carry):
        rows = pl.ds(pl.multiple_of(gi * slab, slab), slab)
        lbs = [lb_all[:, hh * HEAD_DIM:(hh + 1) * HEAD_DIM] for hh in heads_in_step]

        hl, ks = [], []
        for hh in heads_in_step:
            fl = f_ref[hh, rows, :]
            f = lbs[hh] + (1.0 - lbs[hh]) * _sigmoid(fl)
            log_f = jnp.log(f)
            ks.append(1.0 - f)
            hi = log_f.astype(BF16)
            lo = (log_f - hi.astype(F32)).astype(BF16)
            hl.append(jnp.concatenate([hi, lo], axis=1))

        bs = []
        for hh in heads_in_step:
            bb = jnp.dot(tril16, hl[hh], preferred_element_type=F32)
            bs.append(bb[:, :HEAD_DIM] + bb[:, HEAD_DIM:])

        q_decs, k_decs, k_states, decays, v16s = [], [], [], [], []
        for hh in heads_in_step:
            b = bs[hh]
            eb = jnp.exp(b)
            q_decs.append((q_ref[hh, rows, :] * eb).astype(BF16))
            k_dec32 = ks[hh] * (1.0 / eb)
            k_decs.append(k_dec32.astype(BF16))
            dec = [jnp.exp(b[ch.stop - 1:ch.stop, :]) for ch in chunks]
            decays.append(dec)
            k_states.append([(k_dec32[ch] * d).astype(BF16) for ch, d in zip(chunks, dec)])
            v16s.append(i_ref[hh, rows, :].astype(BF16))

        scores, updates = [], []
        for hh in heads_in_step:
            v32 = i_ref[hh, rows, :]
            scores.append([lax.dot_general(q_decs[hh][ch], k_decs[hh][ch],
                                           (((1,), (1,)), ((), ())),
                                           preferred_element_type=F32) for ch in chunks])
            updates.append([jnp.dot(v32[ch].T.astype(BF16), k_states[hh][c],
                                    preferred_element_type=F32)
                            for c, ch in enumerate(chunks)])

        o_parts = []
        for hh in heads_in_step:
            o_intra = [jnp.dot(jnp.where(causal, scores[hh][c], 0.0).astype(BF16),
                               v16s[hh][ch], preferred_element_type=F32)
                       for c, ch in enumerate(chunks)]
            st_t = st_ref[hh]
            o_inter = []
            for c, ch in enumerate(chunks):
                o_inter.append(lax.dot_general(q_decs[hh][ch], st_t.astype(BF16),
                                               (((1,), (1,)), ((), ())),
                                               preferred_element_type=F32))
                st_t = st_t * decays[hh][c] + updates[hh][c]
            st_ref[hh] = st_t
            o_parts.append(jnp.concatenate([a + b for a, b in zip(o_intra, o_inter)], axis=0))

        for hh in heads_in_step:
            o = o_parts[hh]
            gate = g_ref[hh, rows, :]
            nw = nw_all[:, hh * HEAD_DIM:(hh + 1) * HEAD_DIM]
            o = o * lax.rsqrt(jnp.mean(o * o, axis=-1, keepdims=True) + RMS_EPS) * nw
            o = o * (gate * _sigmoid(gate))
            o_ref[0, hh, rows, :] = o.astype(o_ref.dtype)

        for hh in heads_in_step:
            cw = cw_ref[:, hh * LANES:(hh + 1) * LANES]
            u_in = cc_ref[hh, rows, :] * ch_ref[hh, rows, :]
            halo = halo_ref[hh]
            y = cw[CONV_K - 1:CONV_K, :] * u_in
            for tap in range(CONV_K - 1):
                shift = CONV_K - 1 - tap
                shifted = pltpu.roll(u_in, shift, axis=0)
                top = shifted[:SUBLANES]
                for r0 in range(shift):
                    hrow = SUBLANES - shift + r0
                    top = jnp.where(top_row == r0, halo[hrow:hrow + 1, :], top)
                shifted = jnp.concatenate([top, shifted[SUBLANES:]], axis=0)
                y = y + cw[tap:tap + 1, :] * shifted
            halo_ref[hh] = u_in[slab - SUBLANES:, :]
            o_ref[1, hh, rows, :] = (cb_ref[hh, rows, :] * y).astype(o_ref.dtype)
        return carry

    st_ref[...] = jnp.zeros_like(st_ref)
    halo_ref[...] = jnp.zeros_like(halo_ref)
    lax.fori_loop(0, n_chunks // group, slab_step, 0)


def _mixer(proj, lb_logits, out_norm_w, conv_w, *, layer, batch, heads, group=4):
    n_blk, m, lanes = proj.shape
    assert lanes == LANES == HEAD_DIM and n_blk == N_SECTIONS * heads and heads % PAIR == 0
    seq = m // batch
    n_pairs = heads // PAIR
    pw = PAIR * LANES
    assert seq % (CHUNK * group) == 0

    def col(section):
        return pl.BlockSpec((PAIR, seq, LANES),
                            lambda b, p, s=section: (s * n_pairs + p, b, 0))

    def vec(rows):
        return pl.BlockSpec((rows, pw), lambda b, p: (0, p))

    return pl.pallas_call(
        functools.partial(_mixer_kernel, layer=layer, group=group),
        grid=(batch, n_pairs),
        in_specs=[col(s) for s in range(N_SECTIONS)]
                 + [vec(lb_logits.shape[0]), vec(1), vec(CONV_K)],
        out_specs=pl.BlockSpec((2, PAIR, seq, LANES), lambda b, p: (0, p, b, 0)),
        out_shape=jax.ShapeDtypeStruct((2, heads, m, LANES), BF16),
        scratch_shapes=[pltpu.VMEM((PAIR, HEAD_DIM, HEAD_DIM), F32),
                        pltpu.VMEM((PAIR, SUBLANES, LANES), F32)],
        compiler_params=pltpu.CompilerParams(
            dimension_semantics=("parallel", "parallel"),
            vmem_limit_bytes=_vmem_limit(
                [((PAIR, seq, LANES), F32)] * N_SECTIONS + [((2, PAIR, seq, LANES), BF16)],
                [((PAIR, HEAD_DIM, HEAD_DIM), F32)])),
        name="mixer",
    )(*([proj] * N_SECTIONS), lb_logits, out_norm_w, conv_w)


def _outproj_kernel(oc_ref, x_ref, w_ref, o_ref, lhs_ref):
    for k in range(oc_ref.shape[0]):
        lhs_ref[:, k * LANES:(k + 1) * LANES] = oc_ref[k]
    o_ref[...] = x_ref[...] + jnp.dot(lhs_ref[...], w_ref[...], preferred_element_type=F32)


def _out_proj(oc, x, w_out, *, tm=512):
    n_blk, m, _ = oc.shape
    d_mix, d = w_out.shape
    assert n_blk * LANES == d_mix and m % tm == 0
    return pl.pallas_call(
        _outproj_kernel,
        grid=(m // tm,),
        in_specs=[
            pl.BlockSpec((n_blk, tm, LANES), lambda i: (0, i, 0)),
            pl.BlockSpec((tm, d), lambda i: (i, 0)),
            pl.BlockSpec((d_mix, d), lambda i: (0, 0)),
        ],
        out_specs=pl.BlockSpec((tm, d), lambda i: (i, 0)),
        out_shape=jax.ShapeDtypeStruct((m, d), F32),
        scratch_shapes=[pltpu.VMEM((tm, d_mix), BF16)],
        compiler_params=pltpu.CompilerParams(
            dimension_semantics=("parallel",),
            vmem_limit_bytes=_vmem_limit(
                [((n_blk, tm, LANES), BF16), ((tm, d), F32), ((d_mix, d), BF16), ((tm, d), F32)],
                [((tm, d_mix), BF16)])),
        name="out_proj",
    )(oc, x, w_out)


def kernel(x, ffn1_norm_w, ffn1_w_gate, ffn1_w_up, ffn1_w_down, mix_norm_w, w_in_mix,
           hgrn_lb_logits, hgrn_out_norm_w, conv_w, w_out_mix, ffn2_norm_w, ffn2_w_gate,
           ffn2_w_up, ffn2_w_down, final_norm_w):
    batch, seq, d = x.shape
    depth = ffn1_norm_w.shape[0]
    m = batch * seq
    heads = hgrn_out_norm_w.shape[1] // HEAD_DIM
    assert depth >= 1 and conv_w.shape[2] == heads * LANES
    final_w = final_norm_w.reshape(1, d)

    y = x.reshape(m, d)
    for l in range(depth):
        last = l == depth - 1
        y, w_in, w_out = _ffn(
            y, ffn1_norm_w[l].reshape(1, d), ffn1_w_gate[l].astype(BF16),
            ffn1_w_up[l].astype(BF16), ffn1_w_down[l].astype(BF16), final_w,
            final_norm=False, cast_weights=(w_in_mix[l], w_out_mix[l]))
        proj, w2_gate, w2_up, w2_down = _in_proj(
            y, mix_norm_w[l].reshape(1, d), w_in,
            cast_weights=(ffn2_w_gate[l], ffn2_w_up[l], ffn2_w_down[l]))
        oc = _mixer(proj, hgrn_lb_logits, hgrn_out_norm_w[l].reshape(1, -1), conv_w[l],
                    layer=l, batch=batch, heads=heads)
        y = _out_proj(oc.reshape(2 * heads, m, LANES), y, w_out)
        (y,) = _ffn(y, ffn2_norm_w[l].reshape(1, d), w2_gate, w2_up, w2_down, final_w,
                    final_norm=last)
    return y.reshape(batch, seq, d)
```

```python
import functools
import math

import jax
import jax.numpy as jnp
from jax import lax
from jax.experimental import pallas as pl
from jax.experimental.pallas import tpu as pltpu

LANES = 128
SUBLANES = 8
CHUNK = 64
HEAD_DIM = 128
PAIR = 2
N_SECTIONS = 7
CONV_K = 3
RMS_EPS = 1e-6
FFN_RESIDUAL_SCALE = 0.5
MIB = 1024 * 1024
BF16_SUBLANES = 16
V7X_VMEM_BYTES = 64 * MIB
VMEM_RESERVED_BYTES = 4 * MIB
VMEM_COMPILER_TEMP_BYTES = 8 * MIB
ROW_CHUNK = 16
ROW_CHUNK_UNROLL = 8

F32 = jnp.float32
BF16 = jnp.bfloat16


def _rmsnorm_rows(x, w):
    return x * lax.rsqrt(jnp.mean(x * x, axis=-1, keepdims=True) + RMS_EPS) * w


def _for_row_chunks(n_rows, fn):
    def body(r, carry):
        fn(pl.ds(pl.multiple_of(r * ROW_CHUNK, ROW_CHUNK), ROW_CHUNK))
        return carry
    lax.fori_loop(0, n_rows // ROW_CHUNK, body, 0, unroll=ROW_CHUNK_UNROLL)


def _sigmoid(x):
    return 1.0 / (1.0 + jnp.exp(-x))


def _nbytes(shape, dtype):
    return math.prod(shape) * jnp.dtype(dtype).itemsize


def _vmem_limit(pipelined, resident):
    estimate = (2 * sum(_nbytes(s, t) for s, t in pipelined)
                + sum(_nbytes(s, t) for s, t in resident) + VMEM_COMPILER_TEMP_BYTES)
    return min(estimate, V7X_VMEM_BYTES - VMEM_RESERVED_BYTES)


def _cast_rows_per_block(rows, n_steps):
    rpb = BF16_SUBLANES
    while rows % rpb or rows // rpb > n_steps:
        rpb += BF16_SUBLANES
        assert rpb <= rows, "no row blocking of this weight fits the grid"
    return rpb


def _cast_specs(weights, grid):
    n_inner = math.prod(grid[1:])
    specs, shapes = [], []
    for w in weights:
        rows, cols = w.shape
        rpb = _cast_rows_per_block(rows, math.prod(grid))
        last = rows // rpb - 1

        def index_map(i, *inner, last=last):
            step = i * n_inner + (inner[0] if inner else 0)
            return (jnp.minimum(step, last), 0)

        specs.append(pl.BlockSpec((rpb, cols), index_map))
        shapes.append(jax.ShapeDtypeStruct((rows, cols), BF16))
    return specs, shapes


def _run_casts(src_refs, dst_refs):
    for src, dst in zip(src_refs, dst_refs):
        dst[...] = src[...].astype(dst.dtype)


def _ffn_kernel(x_ref, nw_ref, wg_ref, wu_ref, wd_ref, fw_ref, *rest, final_norm, n_cast):
    cast_src, (o_ref, *cast_dst), (h_ref, r_ref) = (
        rest[:n_cast], rest[n_cast:2 * n_cast + 1], rest[2 * n_cast + 1:])
    j = pl.program_id(1)
    tm, d = x_ref.shape

    @pl.when(j == 0)
    def _():
        def norm_rows(rows):
            h_ref[rows, :] = _rmsnorm_rows(x_ref[rows, :], nw_ref[...]).astype(BF16)
            o_ref[rows, :] = jnp.zeros((ROW_CHUNK, d), F32)
        _for_row_chunks(tm, norm_rows)

    h = h_ref[...]
    g = jnp.dot(h, wg_ref[...], preferred_element_type=F32)
    u = jnp.dot(h, wu_ref[...], preferred_element_type=F32)
    _run_casts(cast_src, cast_dst)
    a = (g * _sigmoid(g)) * u
    o_ref[...] += jnp.dot(a.astype(BF16), wd_ref[...], preferred_element_type=F32)

    @pl.when(j == pl.num_programs(1) - 1)
    def _():
        def residual(rows):
            return x_ref[rows, :] + FFN_RESIDUAL_SCALE * o_ref[rows, :]

        if final_norm:
            def row_stats(rows):
                y = residual(rows)
                ms = jnp.mean(y * y, axis=-1, keepdims=True)
                r_ref[rows, :] = jnp.broadcast_to(lax.rsqrt(ms + RMS_EPS), (ROW_CHUNK, LANES))
            _for_row_chunks(tm, row_stats)

        def finish_rows(rows):
            y = residual(rows)
            if final_norm:
                r = r_ref[rows, :]
                y = y * jnp.concatenate([r] * (d // LANES), axis=1) * fw_ref[...]
            o_ref[rows, :] = y
        _for_row_chunks(tm, finish_rows)


def _ffn(x, norm_w, w_gate, w_up, w_down, final_w, *, final_norm, cast_weights=(),
         tm=1024, tf=512):
    m, d = x.shape
    d_ff = w_gate.shape[1]
    assert m % tm == 0 and d_ff % tf == 0
    grid = (m // tm, d_ff // tf)
    cast_specs, cast_shapes = _cast_specs(cast_weights, grid)
    windows = ([((tm, d), F32)] * 2 + [((d, tf), BF16)] * 3
               + [(s.block_shape, F32) for s in cast_specs]
               + [(s.block_shape, BF16) for s in cast_specs])
    scratch = [((tm, d), BF16), ((tm, LANES), F32)]
    return pl.pallas_call(
        functools.partial(_ffn_kernel, final_norm=final_norm, n_cast=len(cast_specs)),
        grid=grid,
        in_specs=[
            pl.BlockSpec((tm, d), lambda i, j: (i, 0)),
            pl.BlockSpec((1, d), lambda i, j: (0, 0)),
            pl.BlockSpec((d, tf), lambda i, j: (0, j)),
            pl.BlockSpec((d, tf), lambda i, j: (0, j)),
            pl.BlockSpec((tf, d), lambda i, j: (j, 0)),
            pl.BlockSpec((1, d), lambda i, j: (0, 0)),
        ] + cast_specs,
        out_specs=[pl.BlockSpec((tm, d), lambda i, j: (i, 0))] + cast_specs,
        out_shape=[jax.ShapeDtypeStruct((m, d), F32)] + cast_shapes,
        scratch_shapes=[pltpu.VMEM(s, t) for s, t in scratch],
        compiler_params=pltpu.CompilerParams(
            dimension_semantics=("arbitrary", "arbitrary"),
            vmem_limit_bytes=_vmem_limit(windows, scratch)),
        name="ffn",
    )(x, norm_w, w_gate, w_up, w_down, final_w, *cast_weights)


def _inproj_kernel(x_ref, nw_ref, wa_ref, wb_ref, *rest, n_cast, tn):
    cast_src, (o_ref, *cast_dst), (h_ref,) = (
        rest[:n_cast], rest[n_cast:2 * n_cast + 1], rest[2 * n_cast + 1:])
    j = pl.program_id(1)

    def project(w_ref):
        for c in range(w_ref.shape[1] // tn):
            p = jnp.dot(h_ref[...], w_ref[:, c * tn:(c + 1) * tn], preferred_element_type=F32)
            for k in range(tn // LANES):
                o_ref[c * (tn // LANES) + k] = p[:, k * LANES:(k + 1) * LANES]

    @pl.when(j == 0)
    def _():
        h_ref[...] = _rmsnorm_rows(x_ref[...], nw_ref[...]).astype(BF16)
        _run_casts(cast_src, cast_dst)
        project(wa_ref)

    @pl.when(j == 1)
    def _():
        _run_casts(cast_src, cast_dst)
        project(wb_ref)


def _in_proj(x, norm_w, w_in, *, cast_weights=(), tm=256, tn=512):
    m, d = x.shape
    n = w_in.shape[1]
    half = n // 2
    assert m % tm == 0 and n % 2 == 0 and half % tn == 0 and tn % LANES == 0
    kb = half // LANES
    grid = (m // tm, 2)
    cast_specs, cast_shapes = _cast_specs(cast_weights, grid)
    windows = ([((tm, d), F32), ((kb, tm, LANES), F32)]
               + [(s.block_shape, F32) for s in cast_specs]
               + [(s.block_shape, BF16) for s in cast_specs])
    resident = [((d, half), BF16)] * 2 + [((tm, d), BF16)]
    w_resident = pl.Buffered(1)
    return pl.pallas_call(
        functools.partial(_inproj_kernel, n_cast=len(cast_specs), tn=tn),
        grid=grid,
        in_specs=[
            pl.BlockSpec((tm, d), lambda i, j: (i, 0)),
            pl.BlockSpec((1, d), lambda i, j: (0, 0)),
            pl.BlockSpec((d, half), lambda i, j: (0, 0), pipeline_mode=w_resident),
            pl.BlockSpec((d, half), lambda i, j: (0, 1), pipeline_mode=w_resident),
        ] + cast_specs,
        out_specs=[pl.BlockSpec((kb, tm, LANES), lambda i, j: (j, i, 0))] + cast_specs,
        out_shape=[jax.ShapeDtypeStruct((n // LANES, m, LANES), F32)] + cast_shapes,
        scratch_shapes=[pltpu.VMEM((tm, d), BF16)],
        compiler_params=pltpu.CompilerParams(
            dimension_semantics=("arbitrary", "arbitrary"),
            vmem_limit_bytes=_vmem_limit(windows, resident)),
        name="in_proj",
    )(x, norm_w, w_in, w_in, *cast_weights)


def _mixer_kernel(q_ref, f_ref, i_ref, g_ref, cb_ref, cc_ref, ch_ref, lb_ref, nw_ref, cw_ref,
                  o_ref, st_ref, halo_ref, *, layer, group):
    seq = q_ref.shape[1]
    n_chunks = seq // CHUNK

    logits = lb_ref[...]
    e = jnp.exp(logits - jnp.max(logits, axis=0, keepdims=True))
    lb_all = jnp.sum(e[:layer + 1], axis=0, keepdims=True) / jnp.sum(e, axis=0, keepdims=True)
    nw_all = nw_ref[...]

    slab = group * CHUNK
    causal = (lax.broadcasted_iota(jnp.int32, (CHUNK, CHUNK), 0)
              >= lax.broadcasted_iota(jnp.int32, (CHUNK, CHUNK), 1))
    srow_i = lax.broadcasted_iota(jnp.int32, (slab, slab), 0)
    scol_i = lax.broadcasted_iota(jnp.int32, (slab, slab), 1)
    tril16 = ((srow_i >= scol_i) & (srow_i // CHUNK == scol_i // CHUNK)).astype(BF16)
    top_row = lax.broadcasted_iota(jnp.int32, (SUBLANES, LANES), 0)
    heads_in_step = range(PAIR)
    chunks = [slice(c * CHUNK, (c + 1) * CHUNK) for c in range(group)]

    def slab_step(gi, carry):
        rows = pl.ds(pl.multiple_of(gi * slab, slab), slab)
        lbs = [lb_all[:, hh * HEAD_DIM:(hh + 1) * HEAD_DIM] for hh in heads_in_step]

        hl, ks = [], []
        for hh in heads_in_step:
            fl = f_ref[hh, rows, :]
            f = lbs[hh] + (1.0 - lbs[hh]) * _sigmoid(fl)
            log_f = jnp.log(f)
            ks.append(1.0 - f)
            hi = log_f.astype(BF16)
            lo = (log_f - hi.astype(F32)).astype(BF16)
            hl.append(jnp.concatenate([hi, lo], axis=1))

        bs = []
        for hh in heads_in_step:
            bb = jnp.dot(tril16, hl[hh], preferred_element_type=F32)
            bs.append(bb[:, :HEAD_DIM] + bb[:, HEAD_DIM:])

        q_decs, k_decs, k_states, decays, v16s = [], [], [], [], []
        for hh in heads_in_step:
            b = bs[hh]
            eb = jnp.exp(b)
            q_decs.append((q_ref[hh, rows, :] * eb).astype(BF16))
            k_dec32 = ks[hh] * (1.0 / eb)
            k_decs.append(k_dec32.astype(BF16))
            dec = [jnp.exp(b[ch.stop - 1:ch.stop, :]) for ch in chunks]
            decays.append(dec)
            k_states.append([(k_dec32[ch] * d).astype(BF16) for ch, d in zip(chunks, dec)])
            v16s.append(i_ref[hh, rows, :].astype(BF16))

        scores, updates = [], []
        for hh in heads_in_step:
            v32 = i_ref[hh, rows, :]
            scores.append([lax.dot_general(q_decs[hh][ch], k_decs[hh][ch],
                                           (((1,), (1,)), ((), ())),
                                           preferred_element_type=F32) for ch in chunks])
            updates.append([jnp.dot(v32[ch].T.astype(BF16), k_states[hh][c],
                                    preferred_element_type=F32)
                            for c, ch in enumerate(chunks)])

        o_parts = []
        for hh in heads_in_step:
            o_intra = [jnp.dot(jnp.where(causal, scores[hh][c], 0.0).astype(BF16),
                               v16s[hh][ch], preferred_element_type=F32)
                       for c, ch in enumerate(chunks)]
            st_t = st_ref[hh]
            o_inter = []
            for c, ch in enumerate(chunks):
                o_inter.append(lax.dot_general(q_decs[hh][ch], st_t.astype(BF16),
                                               (((1,), (1,)), ((), ())),
                                               preferred_element_type=F32))
                st_t = st_t * decays[hh][c] + updates[hh][c]
            st_ref[hh] = st_t
            o_parts.append(jnp.concatenate([a + b for a, b in zip(o_intra, o_inter)], axis=0))

        for hh in heads_in_step:
            o = o_parts[hh]
            gate = g_ref[hh, rows, :]
            nw = nw_all[:, hh * HEAD_DIM:(hh + 1) * HEAD_DIM]
            o = o * lax.rsqrt(jnp.mean(o * o, axis=-1, keepdims=True) + RMS_EPS) * nw
            o = o * (gate * _sigmoid(gate))
            o_ref[0, hh, rows, :] = o.astype(o_ref.dtype)

        for hh in heads_in_step:
            cw = cw_ref[:, hh * LANES:(hh + 1) * LANES]
            u_in = cc_ref[hh, rows, :] * ch_ref[hh, rows, :]
            halo = halo_ref[hh]
            y = cw[CONV_K - 1:CONV_K, :] * u_in
            for tap in range(CONV_K - 1):
                shift = CONV_K - 1 - tap
                shifted = pltpu.roll(u_in, shift, axis=0)
                top = shifted[:SUBLANES]
                for r0 in range(shift):
                    hrow = SUBLANES - shift + r0
                    top = jnp.where(top_row == r0, halo[hrow:hrow + 1, :], top)
                shifted = jnp.concatenate([top, shifted[SUBLANES:]], axis=0)
                y = y + cw[tap:tap + 1, :] * shifted
            halo_ref[hh] = u_in[slab - SUBLANES:, :]
            o_ref[1, hh, rows, :] = (cb_ref[hh, rows, :] * y).astype(o_ref.dtype)
        return carry

    st_ref[...] = jnp.zeros_like(st_ref)
    halo_ref[...] = jnp.zeros_like(halo_ref)
    lax.fori_loop(0, n_chunks // group, slab_step, 0)


def _mixer(proj, lb_logits, out_norm_w, conv_w, *, layer, batch, heads, group=8):
    n_blk, m, lanes = proj.shape
    assert lanes == LANES == HEAD_DIM and n_blk == N_SECTIONS * heads and heads % PAIR == 0
    seq = m // batch
    n_pairs = heads // PAIR
    pw = PAIR * LANES
    assert seq % (CHUNK * group) == 0

    def col(section):
        return pl.BlockSpec((PAIR, seq, LANES),
                            lambda b, p, s=section: (s * n_pairs + p, b, 0))

    def vec(rows):
        return pl.BlockSpec((rows, pw), lambda b, p: (0, p))

    return pl.pallas_call(
        functools.partial(_mixer_kernel, layer=layer, group=group),
        grid=(batch, n_pairs),
        in_specs=[col(s) for s in range(N_SECTIONS)]
                 + [vec(lb_logits.shape[0]), vec(1), vec(CONV_K)],
        out_specs=pl.BlockSpec((2, PAIR, seq, LANES), lambda b, p: (0, p, b, 0)),
        out_shape=jax.ShapeDtypeStruct((2, heads, m, LANES), BF16),
        scratch_shapes=[pltpu.VMEM((PAIR, HEAD_DIM, HEAD_DIM), F32),
                        pltpu.VMEM((PAIR, SUBLANES, LANES), F32)],
        compiler_params=pltpu.CompilerParams(
            dimension_semantics=("parallel", "parallel"),
            vmem_limit_bytes=_vmem_limit(
                [((PAIR, seq, LANES), F32)] * N_SECTIONS + [((2, PAIR, seq, LANES), BF16)],
                [((PAIR, HEAD_DIM, HEAD_DIM), F32)])),
        name="mixer",
    )(*([proj] * N_SECTIONS), lb_logits, out_norm_w, conv_w)


def _outproj_kernel(oc_ref, x_ref, w_ref, o_ref, lhs_ref):
    for k in range(oc_ref.shape[0]):
        lhs_ref[:, k * LANES:(k + 1) * LANES] = oc_ref[k]
    o_ref[...] = x_ref[...] + jnp.dot(lhs_ref[...], w_ref[...], preferred_element_type=F32)


def _out_proj(oc, x, w_out, *, tm=512):
    n_blk, m, _ = oc.shape
    d_mix, d = w_out.shape
    assert n_blk * LANES == d_mix and m % tm == 0
    return pl.pallas_call(
        _outproj_kernel,
        grid=(m // tm,),
        in_specs=[
            pl.BlockSpec((n_blk, tm, LANES), lambda i: (0, i, 0)),
            pl.BlockSpec((tm, d), lambda i: (i, 0)),
            pl.BlockSpec((d_mix, d), lambda i: (0, 0)),
        ],
        out_specs=pl.BlockSpec((tm, d), lambda i: (i, 0)),
        out_shape=jax.ShapeDtypeStruct((m, d), F32),
        scratch_shapes=[pltpu.VMEM((tm, d_mix), BF16)],
        compiler_params=pltpu.CompilerParams(
            dimension_semantics=("parallel",),
            vmem_limit_bytes=_vmem_limit(
                [((n_blk, tm, LANES), BF16), ((tm, d), F32), ((d_mix, d), BF16), ((tm, d), F32)],
                [((tm, d_mix), BF16)])),
        name="out_proj",
    )(oc, x, w_out)


def kernel(x, ffn1_norm_w, ffn1_w_gate, ffn1_w_up, ffn1_w_down, mix_norm_w, w_in_mix,
           hgrn_lb_logits, hgrn_out_norm_w, conv_w, w_out_mix, ffn2_norm_w, ffn2_w_gate,
           ffn2_w_up, ffn2_w_down, final_norm_w):
    batch, seq, d = x.shape
    depth = ffn1_norm_w.shape[0]
    m = batch * seq
    heads = hgrn_out_norm_w.shape[1] // HEAD_DIM
    assert depth >= 1 and conv_w.shape[2] == heads * LANES
    final_w = final_norm_w.reshape(1, d)

    y = x.reshape(m, d)
    for l in range(depth):
        last = l == depth - 1
        y, w_in, w_out = _ffn(
            y, ffn1_norm_w[l].reshape(1, d), ffn1_w_gate[l].astype(BF16),
            ffn1_w_up[l].astype(BF16), ffn1_w_down[l].astype(BF16), final_w,
            final_norm=False, cast_weights=(w_in_mix[l], w_out_mix[l]))
        proj, w2_gate, w2_up, w2_down = _in_proj(
            y, mix_norm_w[l].reshape(1, d), w_in,
            cast_weights=(ffn2_w_gate[l], ffn2_w_up[l], ffn2_w_down[l]))
        oc = _mixer(proj, hgrn_lb_logits, hgrn_out_norm_w[l].reshape(1, -1), conv_w[l],
                    layer=l, batch=batch, heads=heads)
        y = _out_proj(oc.reshape(2 * heads, m, LANES), y, w_out)
        (y,) = _ffn(y, ffn2_norm_w[l].reshape(1, d), w2_gate, w2_up, w2_down, final_w,
                    final_norm=last)
    return y.reshape(batch, seq, d)
```

```python
import functools
import math

import jax
import jax.numpy as jnp
from jax import lax
from jax.experimental import pallas as pl
from jax.experimental.pallas import tpu as pltpu

LANES = 128
SUBLANES = 8
CHUNK = 64
HEAD_DIM = 128
PAIR = 2
N_SECTIONS = 7
CONV_K = 3
RMS_EPS = 1e-6
FFN_RESIDUAL_SCALE = 0.5
MIB = 1024 * 1024
BF16_SUBLANES = 16
V7X_VMEM_BYTES = 64 * MIB
VMEM_RESERVED_BYTES = 4 * MIB
VMEM_COMPILER_TEMP_BYTES = 8 * MIB
ROW_CHUNK = 16
ROW_CHUNK_UNROLL = 8

F32 = jnp.float32
BF16 = jnp.bfloat16


def _rmsnorm_rows(x, w):
    return x * lax.rsqrt(jnp.mean(x * x, axis=-1, keepdims=True) + RMS_EPS) * w


def _for_row_chunks(n_rows, fn):
    def body(r, carry):
        fn(pl.ds(pl.multiple_of(r * ROW_CHUNK, ROW_CHUNK), ROW_CHUNK))
        return carry
    lax.fori_loop(0, n_rows // ROW_CHUNK, body, 0, unroll=ROW_CHUNK_UNROLL)


def _sigmoid(x):
    return 1.0 / (1.0 + jnp.exp(-x))


def _nbytes(shape, dtype):
    return math.prod(shape) * jnp.dtype(dtype).itemsize


def _vmem_limit(pipelined, resident):
    estimate = (2 * sum(_nbytes(s, t) for s, t in pipelined)
                + sum(_nbytes(s, t) for s, t in resident) + VMEM_COMPILER_TEMP_BYTES)
    return min(estimate, V7X_VMEM_BYTES - VMEM_RESERVED_BYTES)


def _cast_rows_per_block(rows, n_steps):
    rpb = BF16_SUBLANES
    while rows % rpb or rows // rpb > n_steps:
        rpb += BF16_SUBLANES
        assert rpb <= rows, "no row blocking of this weight fits the grid"
    return rpb


def _cast_specs(weights, grid):
    n_inner = math.prod(grid[1:])
    specs, shapes = [], []
    for w in weights:
        rows, cols = w.shape
        rpb = _cast_rows_per_block(rows, math.prod(grid))
        last = rows // rpb - 1

        def index_map(i, *inner, last=last):
            step = i * n_inner + (inner[0] if inner else 0)
            return (jnp.minimum(step, last), 0)

        specs.append(pl.BlockSpec((rpb, cols), index_map))
        shapes.append(jax.ShapeDtypeStruct((rows, cols), BF16))
    return specs, shapes


def _run_casts(src_refs, dst_refs):
    for src, dst in zip(src_refs, dst_refs):
        dst[...] = src[...].astype(dst.dtype)


def _ffn_kernel(x_ref, nw_ref, wg_ref, wu_ref, wd_ref, fw_ref, *rest, final_norm, n_cast):
    cast_src, (o_ref, *cast_dst), (h_ref, r_ref) = (
        rest[:n_cast], rest[n_cast:2 * n_cast + 1], rest[2 * n_cast + 1:])
    j = pl.program_id(1)
    tm, d = x_ref.shape

    @pl.when(j == 0)
    def _():
        def norm_rows(rows):
            h_ref[rows, :] = _rmsnorm_rows(x_ref[rows, :], nw_ref[...]).astype(BF16)
            o_ref[rows, :] = jnp.zeros((ROW_CHUNK, d), F32)
        _for_row_chunks(tm, norm_rows)

    h = h_ref[...]
    g = jnp.dot(h, wg_ref[...], preferred_element_type=F32)
    u = jnp.dot(h, wu_ref[...], preferred_element_type=F32)
    _run_casts(cast_src, cast_dst)
    a = (g * _sigmoid(g)) * u
    o_ref[...] += jnp.dot(a.astype(BF16), wd_ref[...], preferred_element_type=F32)

    @pl.when(j == pl.num_programs(1) - 1)
    def _():
        def residual(rows):
            return x_ref[rows, :] + FFN_RESIDUAL_SCALE * o_ref[rows, :]

        if final_norm:
            def row_stats(rows):
                y = residual(rows)
                ms = jnp.mean(y * y, axis=-1, keepdims=True)
                r_ref[rows, :] = jnp.broadcast_to(lax.rsqrt(ms + RMS_EPS), (ROW_CHUNK, LANES))
            _for_row_chunks(tm, row_stats)

        def finish_rows(rows):
            y = residual(rows)
            if final_norm:
                r = r_ref[rows, :]
                y = y * jnp.concatenate([r] * (d // LANES), axis=1) * fw_ref[...]
            o_ref[rows, :] = y
        _for_row_chunks(tm, finish_rows)


def _ffn(x, norm_w, w_gate, w_up, w_down, final_w, *, final_norm, cast_weights=(),
         tm=1024, tf=512):
    m, d = x.shape
    d_ff = w_gate.shape[1]
    assert m % tm == 0 and d_ff % tf == 0
    grid = (m // tm, d_ff // tf)
    cast_specs, cast_shapes = _cast_specs(cast_weights, grid)
    windows = ([((tm, d), F32)] * 2 + [((d, tf), BF16)] * 3
               + [(s.block_shape, F32) for s in cast_specs]
               + [(s.block_shape, BF16) for s in cast_specs])
    scratch = [((tm, d), BF16), ((tm, LANES), F32)]
    return pl.pallas_call(
        functools.partial(_ffn_kernel, final_norm=final_norm, n_cast=len(cast_specs)),
        grid=grid,
        in_specs=[
            pl.BlockSpec((tm, d), lambda i, j: (i, 0)),
            pl.BlockSpec((1, d), lambda i, j: (0, 0)),
            pl.BlockSpec((d, tf), lambda i, j: (0, j)),
            pl.BlockSpec((d, tf), lambda i, j: (0, j)),
            pl.BlockSpec((tf, d), lambda i, j: (j, 0)),
            pl.BlockSpec((1, d), lambda i, j: (0, 0)),
        ] + cast_specs,
        out_specs=[pl.BlockSpec((tm, d), lambda i, j: (i, 0))] + cast_specs,
        out_shape=[jax.ShapeDtypeStruct((m, d), F32)] + cast_shapes,
        scratch_shapes=[pltpu.VMEM(s, t) for s, t in scratch],
        compiler_params=pltpu.CompilerParams(
            dimension_semantics=("arbitrary", "arbitrary"),
            vmem_limit_bytes=_vmem_limit(windows, scratch)),
        name="ffn",
    )(x, norm_w, w_gate, w_up, w_down, final_w, *cast_weights)


def _norm_cast_kernel(x_ref, nw_ref, o_ref):
    def norm_rows(rows):
        o_ref[rows, :] = _rmsnorm_rows(x_ref[rows, :], nw_ref[...]).astype(o_ref.dtype)
    _for_row_chunks(x_ref.shape[0], norm_rows)


def _norm_cast(x, norm_w, *, tm=512):
    m, d = x.shape
    assert m % tm == 0
    return pl.pallas_call(
        _norm_cast_kernel,
        grid=(m // tm,),
        in_specs=[pl.BlockSpec((tm, d), lambda i: (i, 0)),
                  pl.BlockSpec((1, d), lambda i: (0, 0))],
        out_specs=pl.BlockSpec((tm, d), lambda i: (i, 0)),
        out_shape=jax.ShapeDtypeStruct((m, d), BF16),
        compiler_params=pltpu.CompilerParams(
            dimension_semantics=("parallel",),
            vmem_limit_bytes=_vmem_limit([((tm, d), F32), ((tm, d), BF16)], [])),
        name="norm_cast",
    )(x, norm_w)


def _mixer_kernel(hn_ref, hnx_ref, wq_ref, wf_ref, wi_ref, wg_ref, wcb_ref, wcc_ref, wch_ref,
                  lb_ref, nw_ref, cw_ref, *rest, layer, group, n_cast):
    cast_src, (oa_ref, ob_ref, *cast_dst), (p_ref, st_ref, halo_ref) = (
        rest[:n_cast], rest[n_cast:2 * n_cast + 2], rest[2 * n_cast + 2:])
    _mixer_body(hn_ref, hnx_ref, (wq_ref, wf_ref, wi_ref, wg_ref, wcb_ref, wcc_ref, wch_ref),
                lb_ref, nw_ref, cw_ref, cast_src, oa_ref, ob_ref, cast_dst, p_ref, st_ref,
                halo_ref, layer=layer, group=group)


def _mixer_body(hn_ref, hnx_ref, w_refs, lb_ref, nw_ref, cw_ref, cast_src, oa_ref, ob_ref,
                cast_dst, p_ref, st_ref, halo_ref, *, layer, group):
    seq = hn_ref.shape[0]
    slab = group * CHUNK
    n_slabs = seq // slab
    SEC_Q, SEC_F, SEC_I, SEC_G, SEC_CB, SEC_CC, SEC_CH = range(N_SECTIONS)

    logits = lb_ref[...]
    e = jnp.exp(logits - jnp.max(logits, axis=0, keepdims=True))
    lb_all = jnp.sum(e[:layer + 1], axis=0, keepdims=True) / jnp.sum(e, axis=0, keepdims=True)
    nw_all = nw_ref[...]

    causal = (lax.broadcasted_iota(jnp.int32, (CHUNK, CHUNK), 0)
              >= lax.broadcasted_iota(jnp.int32, (CHUNK, CHUNK), 1))
    srow_i = lax.broadcasted_iota(jnp.int32, (slab, slab), 0)
    scol_i = lax.broadcasted_iota(jnp.int32, (slab, slab), 1)
    tril16 = ((srow_i >= scol_i) & (srow_i // CHUNK == scol_i // CHUNK)).astype(BF16)
    top_row = lax.broadcasted_iota(jnp.int32, (SUBLANES, LANES), 0)
    heads_in_step = range(PAIR)
    lanes = [slice(hh * HEAD_DIM, (hh + 1) * HEAD_DIM) for hh in heads_in_step]
    chunks = [slice(c * CHUNK, (c + 1) * CHUNK) for c in range(group)]

    def project_tasks(load_rows, slot):
        def task(s):
            p_ref[slot, s] = jnp.dot(load_rows(), w_refs[s][...], preferred_element_type=F32)
        return [functools.partial(task, s) for s in range(N_SECTIONS)]

    def mix_tasks(row0, slot):
        v = {}
        rows = pl.ds(row0, slab)

        def sec(s, hh):
            return p_ref[slot, s, :, lanes[hh]]

        def gates():
            v["hl"], v["k"] = [], []
            for hh in heads_in_step:
                lb = lb_all[:, lanes[hh]]
                f = lb + (1.0 - lb) * _sigmoid(sec(SEC_F, hh))
                log_f = jnp.log(f)
                v["k"].append(1.0 - f)
                hi = log_f.astype(BF16)
                lo = (log_f - hi.astype(F32)).astype(BF16)
                v["hl"].append(jnp.concatenate([hi, lo], axis=1))

        def cumsum():
            v["b"] = []
            for hh in heads_in_step:
                bb = jnp.dot(tril16, v["hl"][hh], preferred_element_type=F32)
                v["b"].append(bb[:, :HEAD_DIM] + bb[:, HEAD_DIM:])

        def decays():
            v["q_dec"], v["k_dec"], v["k_state"], v["decay"], v["v16"] = [], [], [], [], []
            for hh in heads_in_step:
                b = v["b"][hh]
                eb = jnp.exp(b)
                v["q_dec"].append((sec(SEC_Q, hh) * eb).astype(BF16))
                k_dec32 = v["k"][hh] * (1.0 / eb)
                v["k_dec"].append(k_dec32.astype(BF16))
                dec = [jnp.exp(b[ch.stop - 1:ch.stop, :]) for ch in chunks]
                v["decay"].append(dec)
                v["k_state"].append([(k_dec32[ch] * d).astype(BF16)
                                     for ch, d in zip(chunks, dec)])
                v["v16"].append(sec(SEC_I, hh).astype(BF16))

        def chunk_dots():
            v["scores"], v["update"] = [], []
            for hh in heads_in_step:
                v32 = sec(SEC_I, hh)
                v["scores"].append([lax.dot_general(v["q_dec"][hh][ch], v["k_dec"][hh][ch],
                                                    (((1,), (1,)), ((), ())),
                                                    preferred_element_type=F32)
                                    for ch in chunks])
                v["update"].append([jnp.dot(v32[ch].T.astype(BF16), v["k_state"][hh][c],
                                            preferred_element_type=F32)
                                    for c, ch in enumerate(chunks)])

        def read_out():
            v["o"] = []
            for hh in heads_in_step:
                o_intra = [jnp.dot(jnp.where(causal, v["scores"][hh][c], 0.0).astype(BF16),
                                   v["v16"][hh][ch], preferred_element_type=F32)
                           for c, ch in enumerate(chunks)]
                st_t = st_ref[hh]
                o_inter = []
                for c, ch in enumerate(chunks):
                    o_inter.append(lax.dot_general(v["q_dec"][hh][ch], st_t.astype(BF16),
                                                   (((1,), (1,)), ((), ())),
                                                   preferred_element_type=F32))
                    st_t = st_t * v["decay"][hh][c] + v["update"][hh][c]
                st_ref[hh] = st_t
                v["o"].append(jnp.concatenate([a + b for a, b in zip(o_intra, o_inter)], axis=0))

        def norm_gate():
            for hh in heads_in_step:
                o = v["o"][hh]
                gate = sec(SEC_G, hh)
                o = (o * lax.rsqrt(jnp.mean(o * o, axis=-1, keepdims=True) + RMS_EPS)
                     * nw_all[:, lanes[hh]])
                o = o * (gate * _sigmoid(gate))
                oa_ref[rows, lanes[hh]] = o.astype(oa_ref.dtype)

        def conv():
            for hh in heads_in_step:
                cw = cw_ref[:, lanes[hh]]
                u_in = sec(SEC_CC, hh) * sec(SEC_CH, hh)
                halo = halo_ref[hh]
                y = cw[CONV_K - 1:CONV_K, :] * u_in
                for tap in range(CONV_K - 1):
                    shift = CONV_K - 1 - tap
                    shifted = pltpu.roll(u_in, shift, axis=0)
                    top = shifted[:SUBLANES]
                    for r0 in range(shift):
                        hrow = SUBLANES - shift + r0
                        top = jnp.where(top_row == r0, halo[hrow:hrow + 1, :], top)
                    shifted = jnp.concatenate([top, shifted[SUBLANES:]], axis=0)
                    y = y + cw[tap:tap + 1, :] * shifted
                halo_ref[hh] = u_in[slab - SUBLANES:, :]
                ob_ref[rows, lanes[hh]] = (sec(SEC_CB, hh) * y).astype(ob_ref.dtype)

        return [gates, cumsum, decays, chunk_dots, read_out, norm_gate, conv]

    def run_interleaved(mix_list, proj_list):
        n = max(len(mix_list), len(proj_list))
        for t in range(n):
            for tasks in (mix_list, proj_list):
                for task in tasks[t * len(tasks) // n:(t + 1) * len(tasks) // n]:
                    task()

    @pl.when(pl.program_id(1) == 0)
    def _():
        for task in project_tasks(lambda: hn_ref[0:slab, :], 0):
            task()

    st_ref[...] = jnp.zeros_like(st_ref)
    halo_ref[...] = jnp.zeros_like(halo_ref)

    def slab_pair(i, carry):
        row0 = pl.multiple_of(i * (2 * slab), 2 * slab)
        run_interleaved(mix_tasks(row0, 0),
                        project_tasks(lambda: hn_ref[pl.ds(row0 + slab, slab), :], 1))
        run_interleaved(mix_tasks(row0 + slab, 1),
                        project_tasks(lambda: hn_ref[pl.ds(row0 + 2 * slab, slab), :], 0))
        return carry

    lax.fori_loop(0, n_slabs // 2 - 1, slab_pair, 0)
    row0 = (n_slabs - 2) * slab
    run_interleaved(mix_tasks(row0, 0),
                    project_tasks(lambda: hn_ref[row0 + slab:row0 + 2 * slab, :], 1))
    run_interleaved(mix_tasks(row0 + slab, 1), project_tasks(lambda: hnx_ref[...], 0))
    _run_casts(cast_src, cast_dst)


def _mixer(hn, w_in, lb_logits, out_norm_w, conv_w, *, layer, batch, heads, cast_weights=(),
           group=8):
    m, d = hn.shape
    seq = m // batch
    width = heads * HEAD_DIM
    pw = PAIR * HEAD_DIM
    n_pairs = heads // PAIR
    slab = group * CHUNK
    n_slabs = seq // slab
    assert heads % PAIR == 0 and w_in.shape == (d, N_SECTIONS * width)
    assert seq % slab == 0 and n_slabs % 2 == 0 and n_slabs >= 4 and CONV_K - 1 <= SUBLANES

    def w_spec(section):
        return pl.BlockSpec((d, pw), lambda p, b, s=section: (0, s * n_pairs + p),
                            pipeline_mode=pl.Buffered(1))

    def vec_spec(rows):
        return pl.BlockSpec((rows, pw), lambda p, b: (0, p))

    grid = (n_pairs, batch)
    cast_specs, cast_shapes = _cast_specs(cast_weights, grid)
    out_spec = pl.BlockSpec((seq, pw), lambda p, b: (b, p))
    windows = ([((seq, d), BF16), ((slab, d), BF16)] + [((seq, pw), BF16)] * 2
               + [(s.block_shape, F32) for s in cast_specs]
               + [(s.block_shape, BF16) for s in cast_specs])
    slab_proj = ((N_SECTIONS, slab, pw), F32)
    resident = ([((d, pw), BF16)] * N_SECTIONS + [slab_proj] * 3
                + [((PAIR, HEAD_DIM, HEAD_DIM), F32), ((PAIR, SUBLANES, LANES), F32)])
    return pl.pallas_call(
        functools.partial(_mixer_kernel, layer=layer, group=group, n_cast=len(cast_specs)),
        grid=grid,
        in_specs=[pl.BlockSpec((seq, d), lambda p, b: (b, 0)),
                  pl.BlockSpec((slab, d), lambda p, b: (((b + 1) % batch) * n_slabs, 0))]
                 + [w_spec(s) for s in range(N_SECTIONS)]
                 + [vec_spec(lb_logits.shape[0]), vec_spec(1), vec_spec(CONV_K)]
                 + cast_specs,
        out_specs=[out_spec, out_spec] + cast_specs,
        out_shape=[jax.ShapeDtypeStruct((m, width), BF16)] * 2 + cast_shapes,
        scratch_shapes=[pltpu.VMEM((2,) + slab_proj[0], F32),
                        pltpu.VMEM((PAIR, HEAD_DIM, HEAD_DIM), F32),
                        pltpu.VMEM((PAIR, SUBLANES, LANES), F32)],
        compiler_params=pltpu.CompilerParams(
            dimension_semantics=("arbitrary", "arbitrary"),
            vmem_limit_bytes=_vmem_limit(windows, resident)),
        name="mixer",
    )(hn, hn, *([w_in] * N_SECTIONS), lb_logits, out_norm_w, conv_w, *cast_weights)


def _outproj_kernel(oa_ref, ob_ref, x_ref, wa_ref, wb_ref, o_ref):
    o_ref[...] = (x_ref[...]
                  + jnp.dot(oa_ref[...], wa_ref[...], preferred_element_type=F32)
                  + jnp.dot(ob_ref[...], wb_ref[...], preferred_element_type=F32))


def _out_proj(oa, ob, x, w_out, *, tm=512):
    m, width = oa.shape
    d_mix, d = w_out.shape
    assert d_mix == 2 * width and m % tm == 0
    return pl.pallas_call(
        _outproj_kernel,
        grid=(m // tm,),
        in_specs=[
            pl.BlockSpec((tm, width), lambda i: (i, 0)),
            pl.BlockSpec((tm, width), lambda i: (i, 0)),
            pl.BlockSpec((tm, d), lambda i: (i, 0)),
            pl.BlockSpec((width, d), lambda i: (0, 0)),
            pl.BlockSpec((width, d), lambda i: (1, 0)),
        ],
        out_specs=pl.BlockSpec((tm, d), lambda i: (i, 0)),
        out_shape=jax.ShapeDtypeStruct((m, d), F32),
        compiler_params=pltpu.CompilerParams(
            dimension_semantics=("parallel",),
            vmem_limit_bytes=_vmem_limit(
                [((tm, width), BF16)] * 2 + [((tm, d), F32)] * 2 + [((width, d), BF16)] * 2, [])),
        name="out_proj",
    )(oa, ob, x, w_out, w_out)


def kernel(x, ffn1_norm_w, ffn1_w_gate, ffn1_w_up, ffn1_w_down, mix_norm_w, w_in_mix,
           hgrn_lb_logits, hgrn_out_norm_w, conv_w, w_out_mix, ffn2_norm_w, ffn2_w_gate,
           ffn2_w_up, ffn2_w_down, final_norm_w):
    batch, seq, d = x.shape
    depth = ffn1_norm_w.shape[0]
    m = batch * seq
    heads = hgrn_out_norm_w.shape[1] // HEAD_DIM
    assert depth >= 1 and conv_w.shape[2] == heads * LANES
    final_w = final_norm_w.reshape(1, d)

    y = x.reshape(m, d)
    for l in range(depth):
        last = l == depth - 1
        y, w_in, w_out, w2_gate, w2_up = _ffn(
            y, ffn1_norm_w[l].reshape(1, d), ffn1_w_gate[l].astype(BF16),
            ffn1_w_up[l].astype(BF16), ffn1_w_down[l].astype(BF16), final_w,
            final_norm=False,
            cast_weights=(w_in_mix[l], w_out_mix[l], ffn2_w_gate[l], ffn2_w_up[l]))
        hn = _norm_cast(y, mix_norm_w[l].reshape(1, d))
        oa, ob, w2_down = _mixer(
            hn, w_in, hgrn_lb_logits, hgrn_out_norm_w[l].reshape(1, -1), conv_w[l],
            layer=l, batch=batch, heads=heads, cast_weights=(ffn2_w_down[l],))
        y = _out_proj(oa, ob, y, w_out)
        (y,) = _ffn(y, ffn2_norm_w[l].reshape(1, d), w2_gate, w2_up, w2_down, final_w,
                    final_norm=last)
    return y.reshape(batch, seq, d)
```

```python
import functools
import math

import jax
import jax.numpy as jnp
from jax import lax
from jax.experimental import pallas as pl
from jax.experimental.pallas import tpu as pltpu

LANES = 128
SUBLANES = 8
CHUNK = 64
HEAD_DIM = 128
PAIR = 2
N_SECTIONS = 7
CONV_K = 3
RMS_EPS = 1e-6
FFN_RESIDUAL_SCALE = 0.5
MIB = 1024 * 1024
BF16_SUBLANES = 16
V7X_VMEM_BYTES = 64 * MIB
VMEM_RESERVED_BYTES = 4 * MIB
VMEM_COMPILER_TEMP_BYTES = 8 * MIB
ROW_CHUNK = 16
ROW_CHUNK_UNROLL = 8

F32 = jnp.float32
BF16 = jnp.bfloat16


def _rmsnorm_rows(x, w):
    return x * lax.rsqrt(jnp.mean(x * x, axis=-1, keepdims=True) + RMS_EPS) * w


def _for_row_chunks(n_rows, fn):
    def body(r, carry):
        fn(pl.ds(pl.multiple_of(r * ROW_CHUNK, ROW_CHUNK), ROW_CHUNK))
        return carry
    lax.fori_loop(0, n_rows // ROW_CHUNK, body, 0, unroll=ROW_CHUNK_UNROLL)


def _sigmoid(x):
    return 1.0 / (1.0 + jnp.exp(-x))


def _nbytes(shape, dtype):
    return math.prod(shape) * jnp.dtype(dtype).itemsize


def _vmem_limit(pipelined, resident):
    estimate = (2 * sum(_nbytes(s, t) for s, t in pipelined)
                + sum(_nbytes(s, t) for s, t in resident) + VMEM_COMPILER_TEMP_BYTES)
    return min(estimate, V7X_VMEM_BYTES - VMEM_RESERVED_BYTES)


def _cast_rows_per_block(rows, n_steps):
    rpb = BF16_SUBLANES
    while rows % rpb or rows // rpb > n_steps:
        rpb += BF16_SUBLANES
        assert rpb <= rows, "no row blocking of this weight fits the grid"
    return rpb


def _cast_specs(weights, grid):
    n_inner = math.prod(grid[1:])
    specs, shapes = [], []
    for w in weights:
        rows, cols = w.shape
        rpb = _cast_rows_per_block(rows, math.prod(grid))
        last = rows // rpb - 1

        def index_map(i, *inner, last=last):
            step = i * n_inner + (inner[0] if inner else 0)
            return (jnp.minimum(step, last), 0)

        specs.append(pl.BlockSpec((rpb, cols), index_map))
        shapes.append(jax.ShapeDtypeStruct((rows, cols), BF16))
    return specs, shapes


def _run_casts(src_refs, dst_refs):
    for src, dst in zip(src_refs, dst_refs):
        dst[...] = src[...].astype(dst.dtype)


def _ffn_kernel(x_hbm, nw_ref, wg_ref, wu_ref, wd_ref, pw_ref, *rest, post, n_cast):
    n_out = 2 if post == "norm_copy" else 1
    cast_src, outs, cast_dst, (h_ref, r_ref, x_buf, x_sem) = (
        rest[:n_cast], rest[n_cast:n_cast + n_out],
        rest[n_cast + n_out:2 * n_cast + n_out], rest[2 * n_cast + n_out:])
    o_ref = outs[0]
    i, j = pl.program_id(0), pl.program_id(1)
    tm, d = x_buf.shape

    def x_copy(tile):
        return pltpu.make_async_copy(x_hbm.at[pl.ds(tile * tm, tm), :], x_buf, x_sem)

    @pl.when((i == 0) & (j == 0))
    def _():
        x_copy(0).start()

    @pl.when(j == 0)
    def _():
        x_copy(i).wait()

        def norm_rows(rows):
            x = x_buf[rows, :]
            h_ref[rows, :] = _rmsnorm_rows(x, nw_ref[...]).astype(BF16)
            o_ref[rows, :] = x
        _for_row_chunks(tm, norm_rows)

    @pl.when((j == 1) & (i + 1 < pl.num_programs(0)))
    def _():
        x_copy(i + 1).start()

    h = h_ref[...]
    g = jnp.dot(h, wg_ref[...], preferred_element_type=F32)
    u = jnp.dot(h, wu_ref[...], preferred_element_type=F32)
    _run_casts(cast_src, cast_dst)
    a = (g * _sigmoid(g)) * (u * FFN_RESIDUAL_SCALE)
    o_ref[...] += jnp.dot(a.astype(BF16), wd_ref[...], preferred_element_type=F32)

    if post is not None:
        @pl.when(j == pl.num_programs(1) - 1)
        def _():
            if post == "norm_copy":
                def copy_rows(rows):
                    outs[1][rows, :] = _rmsnorm_rows(o_ref[rows, :], pw_ref[...]).astype(BF16)
                _for_row_chunks(tm, copy_rows)
            else:
                def row_stats(rows):
                    y = o_ref[rows, :]
                    ms = jnp.mean(y * y, axis=-1, keepdims=True)
                    r_ref[rows, :] = jnp.broadcast_to(lax.rsqrt(ms + RMS_EPS),
                                                      (ROW_CHUNK, LANES))
                _for_row_chunks(tm, row_stats)

                def scale_rows(rows):
                    r = r_ref[rows, :]
                    o_ref[rows, :] = (o_ref[rows, :] * jnp.concatenate([r] * (d // LANES), axis=1)
                                      * pw_ref[...])
                _for_row_chunks(tm, scale_rows)


def _ffn(x, norm_w, w_gate, w_up, w_down, post_w, *, post, cast_weights=(), tm=1024, tf=512):
    m, d = x.shape
    d_ff = w_gate.shape[1]
    assert m % tm == 0 and d_ff % tf == 0 and post in (None, "norm", "norm_copy")
    grid = (m // tm, d_ff // tf)
    assert grid[1] >= 2, "the x prefetch is started in step 1 of each row tile"
    cast_specs, cast_shapes = _cast_specs(cast_weights, grid)
    row_spec = pl.BlockSpec((tm, d), lambda i, j: (i, 0))
    copy_specs = [row_spec] if post == "norm_copy" else []
    copy_shapes = [jax.ShapeDtypeStruct((m, d), BF16)] if post == "norm_copy" else []
    windows = ([((tm, d), F32)] + [((tm, d), BF16)] * len(copy_specs) + [((d, tf), BF16)] * 3
               + [(s.block_shape, F32) for s in cast_specs]
               + [(s.block_shape, BF16) for s in cast_specs])
    scratch = [((tm, d), BF16), ((tm, LANES), F32), ((tm, d), F32)]
    return pl.pallas_call(
        functools.partial(_ffn_kernel, post=post, n_cast=len(cast_specs)),
        grid=grid,
        in_specs=[
            pl.BlockSpec(memory_space=pl.ANY),
            pl.BlockSpec((1, d), lambda i, j: (0, 0)),
            pl.BlockSpec((d, tf), lambda i, j: (0, j)),
            pl.BlockSpec((d, tf), lambda i, j: (0, j)),
            pl.BlockSpec((tf, d), lambda i, j: (j, 0)),
            pl.BlockSpec((1, d), lambda i, j: (0, 0)),
        ] + cast_specs,
        out_specs=[row_spec] + copy_specs + cast_specs,
        out_shape=[jax.ShapeDtypeStruct((m, d), F32)] + copy_shapes + cast_shapes,
        scratch_shapes=[pltpu.VMEM(s, t) for s, t in scratch] + [pltpu.SemaphoreType.DMA(())],
        compiler_params=pltpu.CompilerParams(
            dimension_semantics=("arbitrary", "arbitrary"),
            vmem_limit_bytes=_vmem_limit(windows, scratch)),
        name="ffn",
    )(x, norm_w, w_gate, w_up, w_down, post_w, *cast_weights)


def _mixer_kernel(hn_ref, hnx_ref, wq_ref, wf_ref, wi_ref, wg_ref, wcb_ref, wcc_ref, wch_ref,
                  lb_ref, nw_ref, cw_ref, *rest, layer, group, n_cast):
    cast_src, (oa_ref, ob_ref, *cast_dst), (p_ref, st_ref, halo_ref) = (
        rest[:n_cast], rest[n_cast:2 * n_cast + 2], rest[2 * n_cast + 2:])
    _mixer_body(hn_ref, hnx_ref, (wq_ref, wf_ref, wi_ref, wg_ref, wcb_ref, wcc_ref, wch_ref),
                lb_ref, nw_ref, cw_ref, cast_src, oa_ref, ob_ref, cast_dst, p_ref, st_ref,
                halo_ref, layer=layer, group=group)


def _mixer_body(hn_ref, hnx_ref, w_refs, lb_ref, nw_ref, cw_ref, cast_src, oa_ref, ob_ref,
                cast_dst, p_ref, st_ref, halo_ref, *, layer, group):
    seq = hn_ref.shape[0]
    slab = group * CHUNK
    n_slabs = seq // slab
    SEC_Q, SEC_F, SEC_I, SEC_G, SEC_CB, SEC_CC, SEC_CH = range(N_SECTIONS)

    logits = lb_ref[...]
    e = jnp.exp(logits - jnp.max(logits, axis=0, keepdims=True))
    lb_all = jnp.sum(e[:layer + 1], axis=0, keepdims=True) / jnp.sum(e, axis=0, keepdims=True)
    nw_all = nw_ref[...]

    causal = (lax.broadcasted_iota(jnp.int32, (CHUNK, CHUNK), 0)
              >= lax.broadcasted_iota(jnp.int32, (CHUNK, CHUNK), 1))
    srow_i = lax.broadcasted_iota(jnp.int32, (slab, slab), 0)
    scol_i = lax.broadcasted_iota(jnp.int32, (slab, slab), 1)
    tril16 = ((srow_i >= scol_i) & (srow_i // CHUNK == scol_i // CHUNK)).astype(BF16)
    top_row = lax.broadcasted_iota(jnp.int32, (SUBLANES, LANES), 0)
    heads_in_step = range(PAIR)
    lanes = [slice(hh * HEAD_DIM, (hh + 1) * HEAD_DIM) for hh in heads_in_step]
    chunks = [slice(c * CHUNK, (c + 1) * CHUNK) for c in range(group)]

    def project_tasks(load_rows, slot):
        def task(s):
            p_ref[slot, s] = jnp.dot(load_rows(), w_refs[s][...], preferred_element_type=F32)
        return [functools.partial(task, s) for s in range(N_SECTIONS)]

    def mix_tasks(row0, slot):
        v = {}
        rows = pl.ds(row0, slab)

        def sec(s, hh):
            return p_ref[slot, s, :, lanes[hh]]

        def gates():
            v["hl"], v["k"] = [], []
            for hh in heads_in_step:
                lb = lb_all[:, lanes[hh]]
                f = lb + (1.0 - lb) * _sigmoid(sec(SEC_F, hh))
                log_f = jnp.log(f)
                v["k"].append(1.0 - f)
                hi = log_f.astype(BF16)
                lo = (log_f - hi.astype(F32)).astype(BF16)
                v["hl"].append(jnp.concatenate([hi, lo], axis=1))

        def cumsum():
            v["b"] = []
            for hh in heads_in_step:
                bb = jnp.dot(tril16, v["hl"][hh], preferred_element_type=F32)
                v["b"].append(bb[:, :HEAD_DIM] + bb[:, HEAD_DIM:])

        def decays():
            v["q_dec"], v["k_dec"], v["k_state"], v["decay"], v["v16"] = [], [], [], [], []
            for hh in heads_in_step:
                b = v["b"][hh]
                eb = jnp.exp(b)
                v["q_dec"].append((sec(SEC_Q, hh) * eb).astype(BF16))
                k_dec32 = v["k"][hh] * (1.0 / eb)
                v["k_dec"].append(k_dec32.astype(BF16))
                dec = [jnp.exp(b[ch.stop - 1:ch.stop, :]) for ch in chunks]
                v["decay"].append(dec)
                v["k_state"].append([(k_dec32[ch] * d).astype(BF16)
                                     for ch, d in zip(chunks, dec)])
                v["v16"].append(sec(SEC_I, hh).astype(BF16))

        def chunk_dots():
            v["scores"], v["update"] = [], []
            for hh in heads_in_step:
                v32 = sec(SEC_I, hh)
                v["scores"].append([lax.dot_general(v["q_dec"][hh][ch], v["k_dec"][hh][ch],
                                                    (((1,), (1,)), ((), ())),
                                                    preferred_element_type=F32)
                                    for ch in chunks])
                v["update"].append([jnp.dot(v32[ch].T.astype(BF16), v["k_state"][hh][c],
                                            preferred_element_type=F32)
                                    for c, ch in enumerate(chunks)])

        def read_out():
            v["o"] = []
            for hh in heads_in_step:
                o_intra = [jnp.dot(jnp.where(causal, v["scores"][hh][c], 0.0).astype(BF16),
                                   v["v16"][hh][ch], preferred_element_type=F32)
                           for c, ch in enumerate(chunks)]
                st_t = st_ref[hh]
                o_inter = []
                for c, ch in enumerate(chunks):
                    o_inter.append(lax.dot_general(v["q_dec"][hh][ch], st_t.astype(BF16),
                                                   (((1,), (1,)), ((), ())),
                                                   preferred_element_type=F32))
                    st_t = st_t * v["decay"][hh][c] + v["update"][hh][c]
                st_ref[hh] = st_t
                v["o"].append(jnp.concatenate([a + b for a, b in zip(o_intra, o_inter)], axis=0))

        def norm_gate():
            for hh in heads_in_step:
                o = v["o"][hh]
                gate = sec(SEC_G, hh)
                o = (o * lax.rsqrt(jnp.mean(o * o, axis=-1, keepdims=True) + RMS_EPS)
                     * nw_all[:, lanes[hh]])
                o = o * (gate * _sigmoid(gate))
                oa_ref[rows, lanes[hh]] = o.astype(oa_ref.dtype)

        def conv():
            for hh in heads_in_step:
                cw = cw_ref[:, lanes[hh]]
                u_in = sec(SEC_CC, hh) * sec(SEC_CH, hh)
                halo = halo_ref[hh]
                y = cw[CONV_K - 1:CONV_K, :] * u_in
                for tap in range(CONV_K - 1):
                    shift = CONV_K - 1 - tap
                    shifted = pltpu.roll(u_in, shift, axis=0)
                    top = shifted[:SUBLANES]
                    for r0 in range(shift):
                        hrow = SUBLANES - shift + r0
                        top = jnp.where(top_row == r0, halo[hrow:hrow + 1, :], top)
                    shifted = jnp.concatenate([top, shifted[SUBLANES:]], axis=0)
                    y = y + cw[tap:tap + 1, :] * shifted
                halo_ref[hh] = u_in[slab - SUBLANES:, :]
                ob_ref[rows, lanes[hh]] = (sec(SEC_CB, hh) * y).astype(ob_ref.dtype)

        return [gates, cumsum, decays, chunk_dots, read_out, norm_gate, conv]

    def run_interleaved(mix_list, proj_list):
        n = max(len(mix_list), len(proj_list))
        for t in range(n):
            for tasks in (mix_list, proj_list):
                for task in tasks[t * len(tasks) // n:(t + 1) * len(tasks) // n]:
                    task()

    @pl.when(pl.program_id(1) == 0)
    def _():
        for task in project_tasks(lambda: hn_ref[0:slab, :], 0):
            task()

    st_ref[...] = jnp.zeros_like(st_ref)
    halo_ref[...] = jnp.zeros_like(halo_ref)

    def slab_pair(i, carry):
        row0 = pl.multiple_of(i * (2 * slab), 2 * slab)
        run_interleaved(mix_tasks(row0, 0),
                        project_tasks(lambda: hn_ref[pl.ds(row0 + slab, slab), :], 1))
        run_interleaved(mix_tasks(row0 + slab, 1),
                        project_tasks(lambda: hn_ref[pl.ds(row0 + 2 * slab, slab), :], 0))
        return carry

    lax.fori_loop(0, n_slabs // 2 - 1, slab_pair, 0)
    row0 = (n_slabs - 2) * slab
    run_interleaved(mix_tasks(row0, 0),
                    project_tasks(lambda: hn_ref[row0 + slab:row0 + 2 * slab, :], 1))
    run_interleaved(mix_tasks(row0 + slab, 1), project_tasks(lambda: hnx_ref[...], 0))
    _run_casts(cast_src, cast_dst)


def _mixer(hn, w_in, lb_logits, out_norm_w, conv_w, *, layer, batch, heads, cast_weights=(),
           group=8):
    m, d = hn.shape
    seq = m // batch
    width = heads * HEAD_DIM
    pw = PAIR * HEAD_DIM
    n_pairs = heads // PAIR
    slab = group * CHUNK
    n_slabs = seq // slab
    assert heads % PAIR == 0 and w_in.shape == (d, N_SECTIONS * width)
    assert seq % slab == 0 and n_slabs % 2 == 0 and n_slabs >= 4 and CONV_K - 1 <= SUBLANES

    def w_spec(section):
        return pl.BlockSpec((d, pw), lambda p, b, s=section: (0, s * n_pairs + p),
                            pipeline_mode=pl.Buffered(1))

    def vec_spec(rows):
        return pl.BlockSpec((rows, pw), lambda p, b: (0, p))

    grid = (n_pairs, batch)
    cast_specs, cast_shapes = _cast_specs(cast_weights, grid)
    out_spec = pl.BlockSpec((seq, pw), lambda p, b: (b, p))
    windows = ([((seq, d), BF16), ((slab, d), BF16)] + [((seq, pw), BF16)] * 2
               + [(s.block_shape, F32) for s in cast_specs]
               + [(s.block_shape, BF16) for s in cast_specs])
    slab_proj = ((N_SECTIONS, slab, pw), F32)
    resident = ([((d, pw), BF16)] * N_SECTIONS + [slab_proj] * 3
                + [((PAIR, HEAD_DIM, HEAD_DIM), F32), ((PAIR, SUBLANES, LANES), F32)])
    return pl.pallas_call(
        functools.partial(_mixer_kernel, layer=layer, group=group, n_cast=len(cast_specs)),
        grid=grid,
        in_specs=[pl.BlockSpec((seq, d), lambda p, b: (b, 0)),
                  pl.BlockSpec((slab, d), lambda p, b: (((b + 1) % batch) * n_slabs, 0))]
                 + [w_spec(s) for s in range(N_SECTIONS)]
                 + [vec_spec(lb_logits.shape[0]), vec_spec(1), vec_spec(CONV_K)]
                 + cast_specs,
        out_specs=[out_spec, out_spec] + cast_specs,
        out_shape=[jax.ShapeDtypeStruct((m, width), BF16)] * 2 + cast_shapes,
        scratch_shapes=[pltpu.VMEM((2,) + slab_proj[0], F32),
                        pltpu.VMEM((PAIR, HEAD_DIM, HEAD_DIM), F32),
                        pltpu.VMEM((PAIR, SUBLANES, LANES), F32)],
        compiler_params=pltpu.CompilerParams(
            dimension_semantics=("arbitrary", "arbitrary"),
            vmem_limit_bytes=_vmem_limit(windows, resident)),
        name="mixer",
    )(hn, hn, *([w_in] * N_SECTIONS), lb_logits, out_norm_w, conv_w, *cast_weights)


def _outproj_kernel(oa_ref, ob_ref, x_ref, wa_ref, wb_ref, o_ref):
    o_ref[...] = (x_ref[...]
                  + jnp.dot(oa_ref[...], wa_ref[...], preferred_element_type=F32)
                  + jnp.dot(ob_ref[...], wb_ref[...], preferred_element_type=F32))


def _out_proj(oa, ob, x, w_out, *, tm=512):
    m, width = oa.shape
    d_mix, d = w_out.shape
    assert d_mix == 2 * width and m % tm == 0
    return pl.pallas_call(
        _outproj_kernel,
        grid=(m // tm,),
        in_specs=[
            pl.BlockSpec((tm, width), lambda i: (i, 0)),
            pl.BlockSpec((tm, width), lambda i: (i, 0)),
            pl.BlockSpec((tm, d), lambda i: (i, 0)),
            pl.BlockSpec((width, d), lambda i: (0, 0)),
            pl.BlockSpec((width, d), lambda i: (1, 0)),
        ],
        out_specs=pl.BlockSpec((tm, d), lambda i: (i, 0)),
        out_shape=jax.ShapeDtypeStruct((m, d), F32),
        compiler_params=pltpu.CompilerParams(
            dimension_semantics=("parallel",),
            vmem_limit_bytes=_vmem_limit(
                [((tm, width), BF16)] * 2 + [((tm, d), F32)] * 2 + [((width, d), BF16)] * 2, [])),
        name="out_proj",
    )(oa, ob, x, w_out, w_out)


def kernel(x, ffn1_norm_w, ffn1_w_gate, ffn1_w_up, ffn1_w_down, mix_norm_w, w_in_mix,
           hgrn_lb_logits, hgrn_out_norm_w, conv_w, w_out_mix, ffn2_norm_w, ffn2_w_gate,
           ffn2_w_up, ffn2_w_down, final_norm_w):
    batch, seq, d = x.shape
    depth = ffn1_norm_w.shape[0]
    m = batch * seq
    heads = hgrn_out_norm_w.shape[1] // HEAD_DIM
    assert depth >= 1 and conv_w.shape[2] == heads * LANES
    final_w = final_norm_w.reshape(1, d)

    y = x.reshape(m, d)
    for l in range(depth):
        last = l == depth - 1
        y, hn, w_in, w_out, w2_gate, w2_up = _ffn(
            y, ffn1_norm_w[l].reshape(1, d), ffn1_w_gate[l].astype(BF16),
            ffn1_w_up[l].astype(BF16), ffn1_w_down[l].astype(BF16), mix_norm_w[l].reshape(1, d),
            post="norm_copy",
            cast_weights=(w_in_mix[l], w_out_mix[l], ffn2_w_gate[l], ffn2_w_up[l]))
        oa, ob, w2_down = _mixer(
            hn, w_in, hgrn_lb_logits, hgrn_out_norm_w[l].reshape(1, -1), conv_w[l],
            layer=l, batch=batch, heads=heads, cast_weights=(ffn2_w_down[l],))
        y = _out_proj(oa, ob, y, w_out)
        (y,) = _ffn(y, ffn2_norm_w[l].reshape(1, d), w2_gate, w2_up, w2_down, final_w,
                    post="norm" if last else None)
    return y.reshape(batch, seq, d)
```

```python
import functools
import math

import jax
import jax.numpy as jnp
from jax import lax
from jax.experimental import pallas as pl
from jax.experimental.pallas import tpu as pltpu

LANES = 128
SUBLANES = 8
CHUNK = 64
HEAD_DIM = 128
PAIR = 2
N_SECTIONS = 7
CONV_K = 3
CUMSUM_ROWS = 256
RMS_EPS = 1e-6
FFN_RESIDUAL_SCALE = 0.5
MIB = 1024 * 1024
BF16_SUBLANES = 16
V7X_VMEM_BYTES = 64 * MIB
VMEM_RESERVED_BYTES = 4 * MIB
VMEM_COMPILER_TEMP_BYTES = 8 * MIB
ROW_CHUNK = 16
ROW_CHUNK_UNROLL = 8

F32 = jnp.float32
BF16 = jnp.bfloat16


def _rmsnorm_rows(x, w):
    return x * lax.rsqrt(jnp.mean(x * x, axis=-1, keepdims=True) + RMS_EPS) * w


def _for_row_chunks(n_rows, fn):
    def body(r, carry):
        fn(pl.ds(pl.multiple_of(r * ROW_CHUNK, ROW_CHUNK), ROW_CHUNK))
        return carry
    lax.fori_loop(0, n_rows // ROW_CHUNK, body, 0, unroll=ROW_CHUNK_UNROLL)


def _sigmoid(x):
    return 1.0 / (1.0 + jnp.exp(-x))


def _nbytes(shape, dtype):
    return math.prod(shape) * jnp.dtype(dtype).itemsize


def _vmem_limit(pipelined, resident):
    estimate = (2 * sum(_nbytes(s, t) for s, t in pipelined)
                + sum(_nbytes(s, t) for s, t in resident) + VMEM_COMPILER_TEMP_BYTES)
    return min(estimate, V7X_VMEM_BYTES - VMEM_RESERVED_BYTES)


def _cast_rows_per_block(rows, n_steps):
    rpb = BF16_SUBLANES
    while rows % rpb or rows // rpb > n_steps:
        rpb += BF16_SUBLANES
        assert rpb <= rows, "no row blocking of this weight fits the grid"
    return rpb


def _cast_specs(weights, grid):
    n_inner = math.prod(grid[1:])
    specs, shapes = [], []
    for w in weights:
        rows, cols = w.shape
        rpb = _cast_rows_per_block(rows, math.prod(grid))
        last = rows // rpb - 1

        def index_map(i, *inner, last=last):
            step = i * n_inner + (inner[0] if inner else 0)
            return (jnp.minimum(step, last), 0)

        specs.append(pl.BlockSpec((rpb, cols), index_map))
        shapes.append(jax.ShapeDtypeStruct((rows, cols), BF16))
    return specs, shapes


def _run_casts(src_refs, dst_refs):
    for src, dst in zip(src_refs, dst_refs):
        dst[...] = src[...].astype(dst.dtype)


def _ffn_kernel(x_hbm, nw_ref, wg_ref, wu_ref, wd_ref, pw_ref, *rest, post, n_cast):
    n_out = 2 if post == "norm_copy" else 1
    cast_src, outs, cast_dst, (h_ref, r_ref, x_buf, x_sem) = (
        rest[:n_cast], rest[n_cast:n_cast + n_out],
        rest[n_cast + n_out:2 * n_cast + n_out], rest[2 * n_cast + n_out:])
    o_ref = outs[0]
    i, j = pl.program_id(0), pl.program_id(1)
    tm, d = x_buf.shape

    def x_copy(tile):
        return pltpu.make_async_copy(x_hbm.at[pl.ds(tile * tm, tm), :], x_buf, x_sem)

    @pl.when((i == 0) & (j == 0))
    def _():
        x_copy(0).start()

    @pl.when(j == 0)
    def _():
        x_copy(i).wait()

        def norm_rows(rows):
            x = x_buf[rows, :]
            h_ref[rows, :] = _rmsnorm_rows(x, nw_ref[...]).astype(BF16)
            o_ref[rows, :] = x
        _for_row_chunks(tm, norm_rows)

    @pl.when((j == 1) & (i + 1 < pl.num_programs(0)))
    def _():
        x_copy(i + 1).start()

    h = h_ref[...]
    g = jnp.dot(h, wg_ref[...], preferred_element_type=F32)
    u = jnp.dot(h, wu_ref[...], preferred_element_type=F32)
    _run_casts(cast_src, cast_dst)
    a = (g * _sigmoid(g)) * (u * FFN_RESIDUAL_SCALE)
    o_ref[...] += jnp.dot(a.astype(BF16), wd_ref[...], preferred_element_type=F32)

    if post is not None:
        @pl.when(j == pl.num_programs(1) - 1)
        def _():
            if post == "norm_copy":
                def copy_rows(rows):
                    outs[1][rows, :] = _rmsnorm_rows(o_ref[rows, :], pw_ref[...]).astype(BF16)
                _for_row_chunks(tm, copy_rows)
            else:
                def row_stats(rows):
                    y = o_ref[rows, :]
                    ms = jnp.mean(y * y, axis=-1, keepdims=True)
                    r_ref[rows, :] = jnp.broadcast_to(lax.rsqrt(ms + RMS_EPS),
                                                      (ROW_CHUNK, LANES))
                _for_row_chunks(tm, row_stats)

                def scale_rows(rows):
                    r = r_ref[rows, :]
                    o_ref[rows, :] = (o_ref[rows, :] * jnp.concatenate([r] * (d // LANES), axis=1)
                                      * pw_ref[...])
                _for_row_chunks(tm, scale_rows)


def _ffn(x, norm_w, w_gate, w_up, w_down, post_w, *, post, cast_weights=(), tm=1024, tf=512):
    m, d = x.shape
    d_ff = w_gate.shape[1]
    assert m % tm == 0 and d_ff % tf == 0 and post in (None, "norm", "norm_copy")
    grid = (m // tm, d_ff // tf)
    assert grid[1] >= 2, "the x prefetch is started in step 1 of each row tile"
    cast_specs, cast_shapes = _cast_specs(cast_weights, grid)
    row_spec = pl.BlockSpec((tm, d), lambda i, j: (i, 0))
    copy_specs = [row_spec] if post == "norm_copy" else []
    copy_shapes = [jax.ShapeDtypeStruct((m, d), BF16)] if post == "norm_copy" else []
    windows = ([((tm, d), F32)] + [((tm, d), BF16)] * len(copy_specs) + [((d, tf), BF16)] * 3
               + [(s.block_shape, F32) for s in cast_specs]
               + [(s.block_shape, BF16) for s in cast_specs])
    scratch = [((tm, d), BF16), ((tm, LANES), F32), ((tm, d), F32)]
    return pl.pallas_call(
        functools.partial(_ffn_kernel, post=post, n_cast=len(cast_specs)),
        grid=grid,
        in_specs=[
            pl.BlockSpec(memory_space=pl.ANY),
            pl.BlockSpec((1, d), lambda i, j: (0, 0)),
            pl.BlockSpec((d, tf), lambda i, j: (0, j)),
            pl.BlockSpec((d, tf), lambda i, j: (0, j)),
            pl.BlockSpec((tf, d), lambda i, j: (j, 0)),
            pl.BlockSpec((1, d), lambda i, j: (0, 0)),
        ] + cast_specs,
        out_specs=[row_spec] + copy_specs + cast_specs,
        out_shape=[jax.ShapeDtypeStruct((m, d), F32)] + copy_shapes + cast_shapes,
        scratch_shapes=[pltpu.VMEM(s, t) for s, t in scratch] + [pltpu.SemaphoreType.DMA(())],
        compiler_params=pltpu.CompilerParams(
            dimension_semantics=("arbitrary", "arbitrary"),
            vmem_limit_bytes=_vmem_limit(windows, scratch)),
        name="ffn",
    )(x, norm_w, w_gate, w_up, w_down, post_w, *cast_weights)


def _mixer_kernel(hn_ref, hnx_ref, wq_ref, wf_ref, wi_ref, wg_ref, wcb_ref, wcc_ref, wch_ref,
                  lb_ref, nw_ref, cw_ref, *rest, layer, group, n_cast):
    cast_src, (oa_ref, ob_ref, *cast_dst), (p_ref, st_ref, halo_ref) = (
        rest[:n_cast], rest[n_cast:2 * n_cast + 2], rest[2 * n_cast + 2:])
    _mixer_body(hn_ref, hnx_ref, (wq_ref, wf_ref, wi_ref, wg_ref, wcb_ref, wcc_ref, wch_ref),
                lb_ref, nw_ref, cw_ref, cast_src, oa_ref, ob_ref, cast_dst, p_ref, st_ref,
                halo_ref, layer=layer, group=group)


def _mixer_body(hn_ref, hnx_ref, w_refs, lb_ref, nw_ref, cw_ref, cast_src, oa_ref, ob_ref,
                cast_dst, p_ref, st_ref, halo_ref, *, layer, group):
    seq = hn_ref.shape[0]
    slab = group * CHUNK
    n_slabs = seq // slab
    SEC_Q, SEC_F, SEC_I, SEC_G, SEC_CB, SEC_CC, SEC_CH = range(N_SECTIONS)

    logits = lb_ref[...]
    e = jnp.exp(logits - jnp.max(logits, axis=0, keepdims=True))
    lb_all = jnp.sum(e[:layer + 1], axis=0, keepdims=True) / jnp.sum(e, axis=0, keepdims=True)
    nw_all = nw_ref[...]

    causal = (lax.broadcasted_iota(jnp.int32, (CHUNK, CHUNK), 0)
              >= lax.broadcasted_iota(jnp.int32, (CHUNK, CHUNK), 1))
    srow_i = lax.broadcasted_iota(jnp.int32, (CUMSUM_ROWS, CUMSUM_ROWS), 0)
    scol_i = lax.broadcasted_iota(jnp.int32, (CUMSUM_ROWS, CUMSUM_ROWS), 1)
    tril16 = ((srow_i >= scol_i) & (srow_i // CHUNK == scol_i // CHUNK)).astype(BF16)
    top_row = lax.broadcasted_iota(jnp.int32, (SUBLANES, LANES), 0)
    heads_in_step = range(PAIR)
    lanes = [slice(hh * HEAD_DIM, (hh + 1) * HEAD_DIM) for hh in heads_in_step]
    chunks = [slice(c * CHUNK, (c + 1) * CHUNK) for c in range(group)]

    def project_tasks(load_rows, slot):
        def task(s):
            p_ref[slot, s] = jnp.dot(load_rows(), w_refs[s][...], preferred_element_type=F32)
        return [functools.partial(task, s) for s in range(N_SECTIONS)]

    def mix_tasks(row0, slot):
        v = {}
        rows = pl.ds(row0, slab)

        def sec(s, hh):
            return p_ref[slot, s, :, lanes[hh]]

        def gates():
            v["hl"], v["k"] = [], []
            for hh in heads_in_step:
                lb = lb_all[:, lanes[hh]]
                f = lb + (1.0 - lb) * _sigmoid(sec(SEC_F, hh))
                log_f = jnp.log(f)
                v["k"].append(1.0 - f)
                hi = log_f.astype(BF16)
                lo = (log_f - hi.astype(F32)).astype(BF16)
                v["hl"].append(jnp.concatenate([hi, lo], axis=1))

        def cumsum():
            v["b"] = []
            for hh in heads_in_step:
                bb = jnp.concatenate(
                    [jnp.dot(tril16, v["hl"][hh][r0:r0 + CUMSUM_ROWS],
                             preferred_element_type=F32)
                     for r0 in range(0, slab, CUMSUM_ROWS)], axis=0)
                v["b"].append(bb[:, :HEAD_DIM] + bb[:, HEAD_DIM:])

        def decays():
            v["q_dec"], v["k_dec"], v["k_state"], v["decay"], v["v16"] = [], [], [], [], []
            for hh in heads_in_step:
                b = v["b"][hh]
                eb = jnp.exp(b)
                v["q_dec"].append((sec(SEC_Q, hh) * eb).astype(BF16))
                k_dec32 = v["k"][hh] * (1.0 / eb)
                v["k_dec"].append(k_dec32.astype(BF16))
                dec = [jnp.exp(b[ch.stop - 1:ch.stop, :]) for ch in chunks]
                v["decay"].append(dec)
                v["k_state"].append([(k_dec32[ch] * d).astype(BF16)
                                     for ch, d in zip(chunks, dec)])
                v["v16"].append(sec(SEC_I, hh).astype(BF16))

        def chunk_dots():
            v["scores"], v["update"] = [], []
            for hh in heads_in_step:
                v32 = sec(SEC_I, hh)
                v["scores"].append([lax.dot_general(v["q_dec"][hh][ch], v["k_dec"][hh][ch],
                                                    (((1,), (1,)), ((), ())),
                                                    preferred_element_type=F32)
                                    for ch in chunks])
                v["update"].append([jnp.dot(v32[ch].T.astype(BF16), v["k_state"][hh][c],
                                            preferred_element_type=F32)
                                    for c, ch in enumerate(chunks)])

        def read_out():
            v["o"] = []
            for hh in heads_in_step:
                o_intra = [jnp.dot(jnp.where(causal, v["scores"][hh][c], 0.0).astype(BF16),
                                   v["v16"][hh][ch], preferred_element_type=F32)
                           for c, ch in enumerate(chunks)]
                st_t = st_ref[hh]
                o_inter = []
                for c, ch in enumerate(chunks):
                    o_inter.append(lax.dot_general(v["q_dec"][hh][ch], st_t.astype(BF16),
                                                   (((1,), (1,)), ((), ())),
                                                   preferred_element_type=F32))
                    st_t = st_t * v["decay"][hh][c] + v["update"][hh][c]
                st_ref[hh] = st_t
                v["o"].append(jnp.concatenate([a + b for a, b in zip(o_intra, o_inter)], axis=0))

        def norm_gate():
            for hh in heads_in_step:
                o = v["o"][hh]
                gate = sec(SEC_G, hh)
                o = (o * lax.rsqrt(jnp.mean(o * o, axis=-1, keepdims=True) + RMS_EPS)
                     * nw_all[:, lanes[hh]])
                o = o * (gate * _sigmoid(gate))
                oa_ref[rows, lanes[hh]] = o.astype(oa_ref.dtype)

        def conv():
            for hh in heads_in_step:
                cw = cw_ref[:, lanes[hh]]
                u_in = sec(SEC_CC, hh) * sec(SEC_CH, hh)
                halo = halo_ref[hh]
                y = cw[CONV_K - 1:CONV_K, :] * u_in
                for tap in range(CONV_K - 1):
                    shift = CONV_K - 1 - tap
                    shifted = pltpu.roll(u_in, shift, axis=0)
                    top = shifted[:SUBLANES]
                    for r0 in range(shift):
                        hrow = SUBLANES - shift + r0
                        top = jnp.where(top_row == r0, halo[hrow:hrow + 1, :], top)
                    shifted = jnp.concatenate([top, shifted[SUBLANES:]], axis=0)
                    y = y + cw[tap:tap + 1, :] * shifted
                halo_ref[hh] = u_in[slab - SUBLANES:, :]
                ob_ref[rows, lanes[hh]] = (sec(SEC_CB, hh) * y).astype(ob_ref.dtype)

        return [gates, cumsum, decays, chunk_dots, read_out, norm_gate, conv]

    def run_interleaved(mix_list, proj_list):
        n = max(len(mix_list), len(proj_list))
        for t in range(n):
            for tasks in (proj_list, mix_list):
                for task in tasks[t * len(tasks) // n:(t + 1) * len(tasks) // n]:
                    task()

    @pl.when(pl.program_id(1) == 0)
    def _():
        for task in project_tasks(lambda: hn_ref[0:slab, :], 0):
            task()

    st_ref[...] = jnp.zeros_like(st_ref)
    halo_ref[...] = jnp.zeros_like(halo_ref)

    def slab_pair(i, carry):
        row0 = pl.multiple_of(i * (2 * slab), 2 * slab)
        run_interleaved(mix_tasks(row0, 0),
                        project_tasks(lambda: hn_ref[pl.ds(row0 + slab, slab), :], 1))
        run_interleaved(mix_tasks(row0 + slab, 1),
                        project_tasks(lambda: hn_ref[pl.ds(row0 + 2 * slab, slab), :], 0))
        return carry

    lax.fori_loop(0, n_slabs // 2 - 1, slab_pair, 0)
    row0 = (n_slabs - 2) * slab
    run_interleaved(mix_tasks(row0, 0),
                    project_tasks(lambda: hn_ref[row0 + slab:row0 + 2 * slab, :], 1))
    run_interleaved(mix_tasks(row0 + slab, 1), project_tasks(lambda: hnx_ref[...], 0))
    _run_casts(cast_src, cast_dst)


def _mixer(hn, w_in, lb_logits, out_norm_w, conv_w, *, layer, batch, heads, cast_weights=(),
           group=8):
    m, d = hn.shape
    seq = m // batch
    width = heads * HEAD_DIM
    pw = PAIR * HEAD_DIM
    n_pairs = heads // PAIR
    slab = group * CHUNK
    n_slabs = seq // slab
    assert heads % PAIR == 0 and w_in.shape == (d, N_SECTIONS * width)
    assert seq % slab == 0 and n_slabs % 2 == 0 and n_slabs >= 4 and CONV_K - 1 <= SUBLANES
    assert slab % CUMSUM_ROWS == 0 and CUMSUM_ROWS % CHUNK == 0

    def w_spec(section):
        return pl.BlockSpec((d, pw), lambda p, b, s=section: (0, s * n_pairs + p),
                            pipeline_mode=pl.Buffered(1))

    def vec_spec(rows):
        return pl.BlockSpec((rows, pw), lambda p, b: (0, p))

    grid = (n_pairs, batch)
    cast_specs, cast_shapes = _cast_specs(cast_weights, grid)
    out_spec = pl.BlockSpec((seq, pw), lambda p, b: (b, p))
    windows = ([((seq, d), BF16), ((slab, d), BF16)] + [((seq, pw), BF16)] * 2
               + [(s.block_shape, F32) for s in cast_specs]
               + [(s.block_shape, BF16) for s in cast_specs])
    slab_proj = ((N_SECTIONS, slab, pw), F32)
    resident = ([((d, pw), BF16)] * N_SECTIONS + [slab_proj] * 3
                + [((PAIR, HEAD_DIM, HEAD_DIM), F32), ((PAIR, SUBLANES, LANES), F32)])
    return pl.pallas_call(
        functools.partial(_mixer_kernel, layer=layer, group=group, n_cast=len(cast_specs)),
        grid=grid,
        in_specs=[pl.BlockSpec((seq, d), lambda p, b: (b, 0)),
                  pl.BlockSpec((slab, d), lambda p, b: (((b + 1) % batch) * n_slabs, 0))]
                 + [w_spec(s) for s in range(N_SECTIONS)]
                 + [vec_spec(lb_logits.shape[0]), vec_spec(1), vec_spec(CONV_K)]
                 + cast_specs,
        out_specs=[out_spec, out_spec] + cast_specs,
        out_shape=[jax.ShapeDtypeStruct((m, width), BF16)] * 2 + cast_shapes,
        scratch_shapes=[pltpu.VMEM((2,) + slab_proj[0], F32),
                        pltpu.VMEM((PAIR, HEAD_DIM, HEAD_DIM), F32),
                        pltpu.VMEM((PAIR, SUBLANES, LANES), F32)],
        compiler_params=pltpu.CompilerParams(
            dimension_semantics=("arbitrary", "arbitrary"),
            vmem_limit_bytes=_vmem_limit(windows, resident)),
        name="mixer",
    )(hn, hn, *([w_in] * N_SECTIONS), lb_logits, out_norm_w, conv_w, *cast_weights)


def _outproj_kernel(oa_ref, ob_ref, x_ref, wa_ref, wb_ref, o_ref):
    o_ref[...] = (x_ref[...]
                  + jnp.dot(oa_ref[...], wa_ref[...], preferred_element_type=F32)
                  + jnp.dot(ob_ref[...], wb_ref[...], preferred_element_type=F32))


def _out_proj(oa, ob, x, w_out, *, tm=512):
    m, width = oa.shape
    d_mix, d = w_out.shape
    assert d_mix == 2 * width and m % tm == 0
    return pl.pallas_call(
        _outproj_kernel,
        grid=(m // tm,),
        in_specs=[
            pl.BlockSpec((tm, width), lambda i: (i, 0)),
            pl.BlockSpec((tm, width), lambda i: (i, 0)),
            pl.BlockSpec((tm, d), lambda i: (i, 0)),
            pl.BlockSpec((width, d), lambda i: (0, 0)),
            pl.BlockSpec((width, d), lambda i: (1, 0)),
        ],
        out_specs=pl.BlockSpec((tm, d), lambda i: (i, 0)),
        out_shape=jax.ShapeDtypeStruct((m, d), F32),
        compiler_params=pltpu.CompilerParams(
            dimension_semantics=("parallel",),
            vmem_limit_bytes=_vmem_limit(
                [((tm, width), BF16)] * 2 + [((tm, d), F32)] * 2 + [((width, d), BF16)] * 2, [])),
        name="out_proj",
    )(oa, ob, x, w_out, w_out)


def kernel(x, ffn1_norm_w, ffn1_w_gate, ffn1_w_up, ffn1_w_down, mix_norm_w, w_in_mix,
           hgrn_lb_logits, hgrn_out_norm_w, conv_w, w_out_mix, ffn2_norm_w, ffn2_w_gate,
           ffn2_w_up, ffn2_w_down, final_norm_w):
    batch, seq, d = x.shape
    depth = ffn1_norm_w.shape[0]
    m = batch * seq
    heads = hgrn_out_norm_w.shape[1] // HEAD_DIM
    assert depth >= 1 and conv_w.shape[2] == heads * LANES
    final_w = final_norm_w.reshape(1, d)

    y = x.reshape(m, d)
    for l in range(depth):
        last = l == depth - 1
        y, hn, w_in, w_out, w2_gate, w2_up = _ffn(
            y, ffn1_norm_w[l].reshape(1, d), ffn1_w_gate[l].astype(BF16),
            ffn1_w_up[l].astype(BF16), ffn1_w_down[l].astype(BF16), mix_norm_w[l].reshape(1, d),
            post="norm_copy",
            cast_weights=(w_in_mix[l], w_out_mix[l], ffn2_w_gate[l], ffn2_w_up[l]))
        oa, ob, w2_down = _mixer(
            hn, w_in, hgrn_lb_logits, hgrn_out_norm_w[l].reshape(1, -1), conv_w[l],
            layer=l, batch=batch, heads=heads, cast_weights=(ffn2_w_down[l],))
        y = _out_proj(oa, ob, y, w_out)
        (y,) = _ffn(y, ffn2_norm_w[l].reshape(1, d), w2_gate, w2_up, w2_down, final_w,
                    post="norm" if last else None)
    return y.reshape(batch, seq, d)
```

```python
import functools
import math

import jax
import jax.numpy as jnp
from jax import lax
from jax.experimental import pallas as pl
from jax.experimental.pallas import tpu as pltpu

LANES = 128
SUBLANES = 8
CHUNK = 64
HEAD_DIM = 128
PAIR = 2
N_SECTIONS = 7
CONV_K = 3
CUMSUM_ROWS = 256
RMS_EPS = 1e-6
FFN_RESIDUAL_SCALE = 0.5
MIB = 1024 * 1024
BF16_SUBLANES = 16
V7X_VMEM_BYTES = 64 * MIB
VMEM_RESERVED_BYTES = 4 * MIB
VMEM_COMPILER_TEMP_BYTES = 8 * MIB
ROW_CHUNK = 16
ROW_CHUNK_UNROLL = 8

F32 = jnp.float32
BF16 = jnp.bfloat16


def _rmsnorm_rows(x, w):
    return x * lax.rsqrt(jnp.mean(x * x, axis=-1, keepdims=True) + RMS_EPS) * w


def _for_row_chunks(n_rows, fn):
    def body(r, carry):
        fn(pl.ds(pl.multiple_of(r * ROW_CHUNK, ROW_CHUNK), ROW_CHUNK))
        return carry
    lax.fori_loop(0, n_rows // ROW_CHUNK, body, 0, unroll=ROW_CHUNK_UNROLL)


def _sigmoid(x):
    return 1.0 / (1.0 + jnp.exp(-x))


def _nbytes(shape, dtype):
    return math.prod(shape) * jnp.dtype(dtype).itemsize


def _vmem_limit(pipelined, resident):
    estimate = (2 * sum(_nbytes(s, t) for s, t in pipelined)
                + sum(_nbytes(s, t) for s, t in resident) + VMEM_COMPILER_TEMP_BYTES)
    return min(estimate, V7X_VMEM_BYTES - VMEM_RESERVED_BYTES)


def _cast_rows_per_block(rows, n_steps):
    rpb = BF16_SUBLANES
    while rows % rpb or rows // rpb > n_steps:
        rpb += BF16_SUBLANES
        assert rpb <= rows, "no row blocking of this weight fits the grid"
    return rpb


def _cast_specs(weights, grid, first_row=0):
    n_inner = math.prod(grid[1:])
    specs, shapes = [], []
    for w in weights:
        rows, cols = w.shape
        rpb = _cast_rows_per_block(rows, (grid[0] - first_row) * n_inner)
        last = rows // rpb - 1

        def index_map(i, *inner, last=last):
            step = (i - first_row) * n_inner + (inner[0] if inner else 0)
            return (jnp.clip(step, 0, last), 0)

        specs.append(pl.BlockSpec((rpb, cols), index_map))
        shapes.append(jax.ShapeDtypeStruct((rows, cols), BF16))
    return specs, shapes


def _run_casts(src_refs, dst_refs):
    for src, dst in zip(src_refs, dst_refs):
        dst[...] = src[...].astype(dst.dtype)


def _ffn_kernel(x_hbm, nw_ref, wg_ref, wu_ref, wd_ref, pw_ref, *rest, post, n_cast, own_cast,
                has_first):
    rest = list(rest)
    y0_hbm = rest.pop(0) if has_first else None
    n_out = 2 if post == "norm_copy" else 1
    n_w16 = 3 if own_cast else 0
    cast_src, rest = rest[:n_cast], rest[n_cast:]
    outs, rest = rest[:n_out], rest[n_out:]
    w16_refs, rest = rest[:n_w16], rest[n_w16:]
    cast_dst, rest = rest[:n_cast], rest[n_cast:]
    h_ref, r_ref, x_buf, x_sem = rest[:4]
    o_ref = outs[0]
    i, j = pl.program_id(0), pl.program_id(1)
    tm, d = x_buf.shape
    first = 1 if has_first else 0
    computed = i >= first

    def x_copy(tile):
        return pltpu.make_async_copy(x_hbm.at[pl.ds(tile * tm, tm), :], x_buf, x_sem)

    if has_first:
        @pl.when((i == 0) & (j == 0))
        def _():
            y0_copy = pltpu.make_async_copy(y0_hbm, o_ref, rest[4])
            y0_copy.start()
            y0_copy.wait()

    @pl.when((i == first) & (j == 0))
    def _():
        x_copy(first).start()

    @pl.when(computed & (j == 0))
    def _():
        x_copy(i).wait()

        def norm_rows(rows):
            x = x_buf[rows, :]
            h_ref[rows, :] = _rmsnorm_rows(x, nw_ref[...]).astype(BF16)
            o_ref[rows, :] = x
        _for_row_chunks(tm, norm_rows)

    @pl.when(computed & (j == 1) & (i + 1 < pl.num_programs(0)))
    def _():
        x_copy(i + 1).start()

    def ffn_step():
        if own_cast:
            wg, wu, wd = (w_ref[...].astype(BF16) for w_ref in (wg_ref, wu_ref, wd_ref))
            for w16_ref, w in zip(w16_refs, (wg, wu, wd)):
                w16_ref[...] = w
        else:
            wg, wu, wd = wg_ref[...], wu_ref[...], wd_ref[...]
        h = h_ref[...]
        g = jnp.dot(h, wg, preferred_element_type=F32)
        u = jnp.dot(h, wu, preferred_element_type=F32)
        _run_casts(cast_src, cast_dst)
        a = (g * _sigmoid(g)) * (u * FFN_RESIDUAL_SCALE)
        o_ref[...] += jnp.dot(a.astype(BF16), wd, preferred_element_type=F32)

    if has_first:
        pl.when(computed)(ffn_step)
    else:
        ffn_step()

    if post is not None:
        @pl.when(j == pl.num_programs(1) - 1)
        def _():
            if post == "norm_copy":
                def copy_rows(rows):
                    outs[1][rows, :] = _rmsnorm_rows(o_ref[rows, :], pw_ref[...]).astype(BF16)
                _for_row_chunks(tm, copy_rows)
            else:
                def row_stats(rows):
                    y = o_ref[rows, :]
                    ms = jnp.mean(y * y, axis=-1, keepdims=True)
                    r_ref[rows, :] = jnp.broadcast_to(lax.rsqrt(ms + RMS_EPS),
                                                      (ROW_CHUNK, LANES))
                _for_row_chunks(tm, row_stats)

                def scale_rows(rows):
                    r = r_ref[rows, :]
                    o_ref[rows, :] = (o_ref[rows, :] * jnp.concatenate([r] * (d // LANES), axis=1)
                                      * pw_ref[...])
                _for_row_chunks(tm, scale_rows)


def _ffn(x, norm_w, w_gate, w_up, w_down, post_w, *, post, cast_weights=(), first_tile=None,
         tm=1024, tf=512):
    m, d = x.shape
    d_ff = w_gate.shape[1]
    assert m % tm == 0 and d_ff % tf == 0 and post in (None, "norm", "norm_copy")
    grid = (m // tm, d_ff // tf)
    assert grid[1] >= 2, "the x prefetch is started in step 1 of each row tile"
    has_first = first_tile is not None
    first = 1 if has_first else 0
    cast_specs, cast_shapes = _cast_specs(cast_weights, grid, first_row=first)
    row_spec = pl.BlockSpec((tm, d), lambda i, j: (i, 0))
    copy_specs = [row_spec] if post == "norm_copy" else []
    copy_shapes = [jax.ShapeDtypeStruct((m, d), BF16)] if post == "norm_copy" else []

    def col(i, j):
        return jnp.where(i < first, 0, j)

    windows = ([((tm, d), F32)] + [((tm, d), BF16)] * len(copy_specs) + [((d, tf), BF16)] * 3
               + [(s.block_shape, F32) for s in cast_specs]
               + [(s.block_shape, BF16) for s in cast_specs])
    scratch = [((tm, d), BF16), ((tm, LANES), F32), ((tm, d), F32)]
    n_sem = 2 if has_first else 1
    return pl.pallas_call(
        functools.partial(_ffn_kernel, post=post, n_cast=len(cast_specs), own_cast=False,
                          has_first=has_first),
        grid=grid,
        in_specs=[
            pl.BlockSpec(memory_space=pl.ANY),
            pl.BlockSpec((1, d), lambda i, j: (0, 0)),
            pl.BlockSpec((d, tf), lambda i, j: (0, col(i, j))),
            pl.BlockSpec((d, tf), lambda i, j: (0, col(i, j))),
            pl.BlockSpec((tf, d), lambda i, j: (col(i, j), 0)),
            pl.BlockSpec((1, d), lambda i, j: (0, 0)),
        ] + [pl.BlockSpec(memory_space=pl.ANY)] * first + cast_specs,
        out_specs=[row_spec] + copy_specs + cast_specs,
        out_shape=[jax.ShapeDtypeStruct((m, d), F32)] + copy_shapes + cast_shapes,
        scratch_shapes=([pltpu.VMEM(s, t) for s, t in scratch]
                        + [pltpu.SemaphoreType.DMA(())] * n_sem),
        compiler_params=pltpu.CompilerParams(
            dimension_semantics=("arbitrary", "arbitrary"),
            vmem_limit_bytes=_vmem_limit(windows, scratch)),
        name="ffn",
    )(x, norm_w, w_gate, w_up, w_down, post_w, *([first_tile] * first), *cast_weights)


def _ffn_head(x, norm_w, w_gate, w_up, w_down, *, tm=1024, tf=256):
    m, d = x.shape
    d_ff = w_gate.shape[1]
    assert m % tm == 0 and d_ff % tf == 0
    grid = (1, d_ff // tf)
    specs = [pl.BlockSpec((d, tf), lambda i, j: (0, j)),
             pl.BlockSpec((d, tf), lambda i, j: (0, j)),
             pl.BlockSpec((tf, d), lambda i, j: (j, 0))]
    windows = [((d, tf), F32)] * 3 + [((d, tf), BF16)] * 3
    tile_spec = pl.BlockSpec((tm, d), lambda i, j: (0, 0), pipeline_mode=pl.Buffered(1))
    scratch = [((tm, d), BF16), ((tm, LANES), F32), ((tm, d), F32), ((tm, d), F32)]
    return pl.pallas_call(
        functools.partial(_ffn_kernel, post=None, n_cast=0, own_cast=True, has_first=False),
        grid=grid,
        in_specs=[pl.BlockSpec(memory_space=pl.ANY),
                  pl.BlockSpec((1, d), lambda i, j: (0, 0))] + specs
                 + [pl.BlockSpec((1, d), lambda i, j: (0, 0))],
        out_specs=[tile_spec] + specs,
        out_shape=[jax.ShapeDtypeStruct((tm, d), F32)]
                  + [jax.ShapeDtypeStruct(w.shape, BF16) for w in (w_gate, w_up, w_down)],
        scratch_shapes=([pltpu.VMEM(s, t) for s, t in scratch[:3]]
                        + [pltpu.SemaphoreType.DMA(())]),
        compiler_params=pltpu.CompilerParams(
            dimension_semantics=("arbitrary", "arbitrary"),
            vmem_limit_bytes=_vmem_limit(windows, scratch)),
        name="ffn_head",
    )(x, norm_w, w_gate, w_up, w_down, norm_w)


def _mixer_kernel(hn_ref, hnx_ref, wq_ref, wf_ref, wi_ref, wg_ref, wcb_ref, wcc_ref, wch_ref,
                  lb_ref, nw_ref, cw_ref, *rest, layer, group, n_cast):
    cast_src, (oa_ref, ob_ref, *cast_dst), (p_ref, st_ref, halo_ref) = (
        rest[:n_cast], rest[n_cast:2 * n_cast + 2], rest[2 * n_cast + 2:])
    _mixer_body(hn_ref, hnx_ref, (wq_ref, wf_ref, wi_ref, wg_ref, wcb_ref, wcc_ref, wch_ref),
                lb_ref, nw_ref, cw_ref, cast_src, oa_ref, ob_ref, cast_dst, p_ref, st_ref,
                halo_ref, layer=layer, group=group)


def _mixer_body(hn_ref, hnx_ref, w_refs, lb_ref, nw_ref, cw_ref, cast_src, oa_ref, ob_ref,
                cast_dst, p_ref, st_ref, halo_ref, *, layer, group):
    seq = hn_ref.shape[0]
    slab = group * CHUNK
    n_slabs = seq // slab
    SEC_Q, SEC_F, SEC_I, SEC_G, SEC_CB, SEC_CC, SEC_CH = range(N_SECTIONS)

    logits = lb_ref[...]
    e = jnp.exp(logits - jnp.max(logits, axis=0, keepdims=True))
    lb_all = jnp.sum(e[:layer + 1], axis=0, keepdims=True) / jnp.sum(e, axis=0, keepdims=True)
    nw_all = nw_ref[...]

    causal = (lax.broadcasted_iota(jnp.int32, (CHUNK, CHUNK), 0)
              >= lax.broadcasted_iota(jnp.int32, (CHUNK, CHUNK), 1))
    srow_i = lax.broadcasted_iota(jnp.int32, (CUMSUM_ROWS, CUMSUM_ROWS), 0)
    scol_i = lax.broadcasted_iota(jnp.int32, (CUMSUM_ROWS, CUMSUM_ROWS), 1)
    tril16 = ((srow_i >= scol_i) & (srow_i // CHUNK == scol_i // CHUNK)).astype(BF16)
    top_row = lax.broadcasted_iota(jnp.int32, (SUBLANES, LANES), 0)
    heads_in_step = range(PAIR)
    lanes = [slice(hh * HEAD_DIM, (hh + 1) * HEAD_DIM) for hh in heads_in_step]
    chunks = [slice(c * CHUNK, (c + 1) * CHUNK) for c in range(group)]

    def project_tasks(load_rows, slot):
        def task(s):
            p_ref[slot, s] = jnp.dot(load_rows(), w_refs[s][...], preferred_element_type=F32)
        return [functools.partial(task, s) for s in range(N_SECTIONS)]

    def mix_tasks(row0, slot):
        v = {}
        rows = pl.ds(row0, slab)

        def sec(s, hh):
            return p_ref[slot, s, :, lanes[hh]]

        def gates():
            v["hl"], v["k"] = [], []
            for hh in heads_in_step:
                lb = lb_all[:, lanes[hh]]
                f = lb + (1.0 - lb) * _sigmoid(sec(SEC_F, hh))
                log_f = jnp.log(f)
                v["k"].append(1.0 - f)
                hi = log_f.astype(BF16)
                lo = (log_f - hi.astype(F32)).astype(BF16)
                v["hl"].append(jnp.concatenate([hi, lo], axis=1))

        def cumsum():
            v["b"] = []
            for hh in heads_in_step:
                bb = jnp.concatenate(
                    [jnp.dot(tril16, v["hl"][hh][r0:r0 + CUMSUM_ROWS],
                             preferred_element_type=F32)
                     for r0 in range(0, slab, CUMSUM_ROWS)], axis=0)
                v["b"].append(bb[:, :HEAD_DIM] + bb[:, HEAD_DIM:])

        def decays():
            v["q_dec"], v["k_dec"], v["k_state"], v["decay"], v["v16"] = [], [], [], [], []
            for hh in heads_in_step:
                b = v["b"][hh]
                eb = jnp.exp(b)
                v["q_dec"].append((sec(SEC_Q, hh) * eb).astype(BF16))
                k_dec32 = v["k"][hh] * (1.0 / eb)
                v["k_dec"].append(k_dec32.astype(BF16))
                dec = [jnp.exp(b[ch.stop - 1:ch.stop, :]) for ch in chunks]
                v["decay"].append(dec)
                v["k_state"].append([(k_dec32[ch] * d).astype(BF16)
                                     for ch, d in zip(chunks, dec)])
                v["v16"].append(sec(SEC_I, hh).astype(BF16))

        def chunk_dots():
            v["scores"], v["update"] = [], []
            for hh in heads_in_step:
                v32 = sec(SEC_I, hh)
                v["scores"].append([lax.dot_general(v["q_dec"][hh][ch], v["k_dec"][hh][ch],
                                                    (((1,), (1,)), ((), ())),
                                                    preferred_element_type=F32)
                                    for ch in chunks])
                v["update"].append([jnp.dot(v32[ch].T.astype(BF16), v["k_state"][hh][c],
                                            preferred_element_type=F32)
                                    for c, ch in enumerate(chunks)])

        def read_out():
            v["o"] = []
            for hh in heads_in_step:
                o_intra = [jnp.dot(jnp.where(causal, v["scores"][hh][c], 0.0).astype(BF16),
                                   v["v16"][hh][ch], preferred_element_type=F32)
                           for c, ch in enumerate(chunks)]
                st_t = st_ref[hh]
                o_inter = []
                for c, ch in enumerate(chunks):
                    o_inter.append(lax.dot_general(v["q_dec"][hh][ch], st_t.astype(BF16),
                                                   (((1,), (1,)), ((), ())),
                                                   preferred_element_type=F32))
                    st_t = st_t * v["decay"][hh][c] + v["update"][hh][c]
                st_ref[hh] = st_t
                v["o"].append(jnp.concatenate([a + b for a, b in zip(o_intra, o_inter)], axis=0))

        def norm_gate():
            for hh in heads_in_step:
                o = v["o"][hh]
                gate = sec(SEC_G, hh)
                o = (o * lax.rsqrt(jnp.mean(o * o, axis=-1, keepdims=True) + RMS_EPS)
                     * nw_all[:, lanes[hh]])
                o = o * (gate * _sigmoid(gate))
                oa_ref[rows, lanes[hh]] = o.astype(oa_ref.dtype)

        def conv():
            for hh in heads_in_step:
                cw = cw_ref[:, lanes[hh]]
                u_in = sec(SEC_CC, hh) * sec(SEC_CH, hh)
                halo = halo_ref[hh]
                y = cw[CONV_K - 1:CONV_K, :] * u_in
                for tap in range(CONV_K - 1):
                    shift = CONV_K - 1 - tap
                    shifted = pltpu.roll(u_in, shift, axis=0)
                    top = shifted[:SUBLANES]
                    for r0 in range(shift):
                        hrow = SUBLANES - shift + r0
                        top = jnp.where(top_row == r0, halo[hrow:hrow + 1, :], top)
                    shifted = jnp.concatenate([top, shifted[SUBLANES:]], axis=0)
                    y = y + cw[tap:tap + 1, :] * shifted
                halo_ref[hh] = u_in[slab - SUBLANES:, :]
                ob_ref[rows, lanes[hh]] = (sec(SEC_CB, hh) * y).astype(ob_ref.dtype)

        return [gates, cumsum, decays, chunk_dots, read_out, norm_gate, conv]

    def run_interleaved(mix_list, proj_list):
        n = max(len(mix_list), len(proj_list))
        for t in range(n):
            for tasks in (proj_list, mix_list):
                for task in tasks[t * len(tasks) // n:(t + 1) * len(tasks) // n]:
                    task()

    @pl.when(pl.program_id(1) == 0)
    def _():
        for task in project_tasks(lambda: hn_ref[0:slab, :], 0):
            task()

    st_ref[...] = jnp.zeros_like(st_ref)
    halo_ref[...] = jnp.zeros_like(halo_ref)

    def slab_pair(i, carry):
        row0 = pl.multiple_of(i * (2 * slab), 2 * slab)
        run_interleaved(mix_tasks(row0, 0),
                        project_tasks(lambda: hn_ref[pl.ds(row0 + slab, slab), :], 1))
        run_interleaved(mix_tasks(row0 + slab, 1),
                        project_tasks(lambda: hn_ref[pl.ds(row0 + 2 * slab, slab), :], 0))
        return carry

    lax.fori_loop(0, n_slabs // 2 - 1, slab_pair, 0)
    row0 = (n_slabs - 2) * slab
    run_interleaved(mix_tasks(row0, 0),
                    project_tasks(lambda: hn_ref[row0 + slab:row0 + 2 * slab, :], 1))
    run_interleaved(mix_tasks(row0 + slab, 1), project_tasks(lambda: hnx_ref[...], 0))
    _run_casts(cast_src, cast_dst)


def _mixer(hn, w_in, lb_logits, out_norm_w, conv_w, *, layer, batch, heads, cast_weights=(),
           group=8):
    m, d = hn.shape
    seq = m // batch
    width = heads * HEAD_DIM
    pw = PAIR * HEAD_DIM
    n_pairs = heads // PAIR
    slab = group * CHUNK
    n_slabs = seq // slab
    assert heads % PAIR == 0 and w_in.shape == (d, N_SECTIONS * width)
    assert seq % slab == 0 and n_slabs % 2 == 0 and n_slabs >= 4 and CONV_K - 1 <= SUBLANES
    assert slab % CUMSUM_ROWS == 0 and CUMSUM_ROWS % CHUNK == 0

    def w_spec(section):
        return pl.BlockSpec((d, pw), lambda p, b, s=section: (0, s * n_pairs + p),
                            pipeline_mode=pl.Buffered(1))

    def vec_spec(rows):
        return pl.BlockSpec((rows, pw), lambda p, b: (0, p))

    grid = (n_pairs, batch)
    cast_specs, cast_shapes = _cast_specs(cast_weights, grid)
    out_spec = pl.BlockSpec((seq, pw), lambda p, b: (b, p))
    windows = ([((seq, d), BF16), ((slab, d), BF16)] + [((seq, pw), BF16)] * 2
               + [(s.block_shape, F32) for s in cast_specs]
               + [(s.block_shape, BF16) for s in cast_specs])
    slab_proj = ((N_SECTIONS, slab, pw), F32)
    resident = ([((d, pw), BF16)] * N_SECTIONS + [slab_proj] * 3
                + [((PAIR, HEAD_DIM, HEAD_DIM), F32), ((PAIR, SUBLANES, LANES), F32)])
    return pl.pallas_call(
        functools.partial(_mixer_kernel, layer=layer, group=group, n_cast=len(cast_specs)),
        grid=grid,
        in_specs=[pl.BlockSpec((seq, d), lambda p, b: (b, 0)),
                  pl.BlockSpec((slab, d), lambda p, b: (((b + 1) % batch) * n_slabs, 0))]
                 + [w_spec(s) for s in range(N_SECTIONS)]
                 + [vec_spec(lb_logits.shape[0]), vec_spec(1), vec_spec(CONV_K)]
                 + cast_specs,
        out_specs=[out_spec, out_spec] + cast_specs,
        out_shape=[jax.ShapeDtypeStruct((m, width), BF16)] * 2 + cast_shapes,
        scratch_shapes=[pltpu.VMEM((2,) + slab_proj[0], F32),
                        pltpu.VMEM((PAIR, HEAD_DIM, HEAD_DIM), F32),
                        pltpu.VMEM((PAIR, SUBLANES, LANES), F32)],
        compiler_params=pltpu.CompilerParams(
            dimension_semantics=("arbitrary", "arbitrary"),
            vmem_limit_bytes=_vmem_limit(windows, resident)),
        name="mixer",
    )(hn, hn, *([w_in] * N_SECTIONS), lb_logits, out_norm_w, conv_w, *cast_weights)


def _outproj_kernel(oa_ref, ob_ref, x_ref, wa_ref, wb_ref, o_ref):
    o_ref[...] = (x_ref[...]
                  + jnp.dot(oa_ref[...], wa_ref[...], preferred_element_type=F32)
                  + jnp.dot(ob_ref[...], wb_ref[...], preferred_element_type=F32))


def _out_proj(oa, ob, x, w_out, *, tm=512):
    m, width = oa.shape
    d_mix, d = w_out.shape
    assert d_mix == 2 * width and m % tm == 0
    return pl.pallas_call(
        _outproj_kernel,
        grid=(m // tm,),
        in_specs=[
            pl.BlockSpec((tm, width), lambda i: (i, 0)),
            pl.BlockSpec((tm, width), lambda i: (i, 0)),
            pl.BlockSpec((tm, d), lambda i: (i, 0)),
            pl.BlockSpec((width, d), lambda i: (0, 0)),
            pl.BlockSpec((width, d), lambda i: (1, 0)),
        ],
        out_specs=pl.BlockSpec((tm, d), lambda i: (i, 0)),
        out_shape=jax.ShapeDtypeStruct((m, d), F32),
        compiler_params=pltpu.CompilerParams(
            dimension_semantics=("parallel",),
            vmem_limit_bytes=_vmem_limit(
                [((tm, width), BF16)] * 2 + [((tm, d), F32)] * 2 + [((width, d), BF16)] * 2, [])),
        name="out_proj",
    )(oa, ob, x, w_out, w_out)


def kernel(x, ffn1_norm_w, ffn1_w_gate, ffn1_w_up, ffn1_w_down, mix_norm_w, w_in_mix,
           hgrn_lb_logits, hgrn_out_norm_w, conv_w, w_out_mix, ffn2_norm_w, ffn2_w_gate,
           ffn2_w_up, ffn2_w_down, final_norm_w):
    batch, seq, d = x.shape
    depth = ffn1_norm_w.shape[0]
    m = batch * seq
    heads = hgrn_out_norm_w.shape[1] // HEAD_DIM
    assert depth >= 1 and conv_w.shape[2] == heads * LANES
    final_w = final_norm_w.reshape(1, d)

    y = x.reshape(m, d)
    for l in range(depth):
        last = l == depth - 1
        y0, w1_gate, w1_up, w1_down = _ffn_head(
            y, ffn1_norm_w[l].reshape(1, d), ffn1_w_gate[l], ffn1_w_up[l], ffn1_w_down[l])
        y, hn, w_in, w_out, w2_gate, w2_up = _ffn(
            y, ffn1_norm_w[l].reshape(1, d), w1_gate, w1_up, w1_down,
            mix_norm_w[l].reshape(1, d), post="norm_copy", first_tile=y0,
            cast_weights=(w_in_mix[l], w_out_mix[l], ffn2_w_gate[l], ffn2_w_up[l]))
        oa, ob, w2_down = _mixer(
            hn, w_in, hgrn_lb_logits, hgrn_out_norm_w[l].reshape(1, -1), conv_w[l],
            layer=l, batch=batch, heads=heads, cast_weights=(ffn2_w_down[l],))
        y = _out_proj(oa, ob, y, w_out)
        (y,) = _ffn(y, ffn2_norm_w[l].reshape(1, d), w2_gate, w2_up, w2_down, final_w,
                    post="norm" if last else None)
    return y.reshape(batch, seq, d)
```

```python
import functools
import math

import jax
import jax.numpy as jnp
from jax import lax
from jax.experimental import pallas as pl
from jax.experimental.pallas import tpu as pltpu

LANES = 128
SUBLANES = 8
CHUNK = 64
HEAD_DIM = 128
PAIR = 2
N_SECTIONS = 7
CONV_K = 3
CUMSUM_ROWS = 256
RMS_EPS = 1e-6
FFN_RESIDUAL_SCALE = 0.5
MIB = 1024 * 1024
BF16_SUBLANES = 16
V7X_VMEM_BYTES = 64 * MIB
VMEM_RESERVED_BYTES = 4 * MIB
VMEM_COMPILER_TEMP_BYTES = 8 * MIB
ROW_CHUNK = 16
ROW_CHUNK_UNROLL = 8

F32 = jnp.float32
BF16 = jnp.bfloat16


def _rmsnorm_rows(x, w):
    return x * lax.rsqrt(jnp.mean(x * x, axis=-1, keepdims=True) + RMS_EPS) * w


def _for_row_chunks(n_rows, fn):
    def body(r, carry):
        fn(pl.ds(pl.multiple_of(r * ROW_CHUNK, ROW_CHUNK), ROW_CHUNK))
        return carry
    lax.fori_loop(0, n_rows // ROW_CHUNK, body, 0, unroll=ROW_CHUNK_UNROLL)


def _sigmoid(x):
    return 1.0 / (1.0 + jnp.exp(-x))


def _nbytes(shape, dtype):
    return math.prod(shape) * jnp.dtype(dtype).itemsize


def _vmem_limit(pipelined, resident):
    estimate = (2 * sum(_nbytes(s, t) for s, t in pipelined)
                + sum(_nbytes(s, t) for s, t in resident) + VMEM_COMPILER_TEMP_BYTES)
    return min(estimate, V7X_VMEM_BYTES - VMEM_RESERVED_BYTES)


def _cast_rows_per_block(rows, n_steps):
    rpb = BF16_SUBLANES
    while rows % rpb or rows // rpb > n_steps:
        rpb += BF16_SUBLANES
        assert rpb <= rows, "no row blocking of this weight fits the grid"
    return rpb


def _cast_specs(weights, grid, first_row=0):
    n_inner = math.prod(grid[1:])
    specs, shapes = [], []
    for w in weights:
        rows, cols = w.shape
        rpb = _cast_rows_per_block(rows, (grid[0] - first_row) * n_inner)
        last = rows // rpb - 1

        def index_map(i, *inner, last=last):
            step = (i - first_row) * n_inner + (inner[0] if inner else 0)
            return (jnp.clip(step, 0, last), 0)

        specs.append(pl.BlockSpec((rpb, cols), index_map))
        shapes.append(jax.ShapeDtypeStruct((rows, cols), BF16))
    return specs, shapes


def _run_casts(src_refs, dst_refs):
    for src, dst in zip(src_refs, dst_refs):
        dst[...] = src[...].astype(dst.dtype)


FFN_DMA_SEMAPHORES = [pltpu.SemaphoreType.DMA(())] * 2

def _ffn_kernel(x_hbm, nw_ref, wg_ref, wu_ref, wd_ref, pw_ref, *rest, post, n_cast, own_cast,
                has_first):
    rest = list(rest)
    y0_hbm = rest.pop(0) if has_first else None
    n_out = 2 if post == "norm_copy" else 1
    n_w16 = 3 if own_cast else 0
    cast_src, rest = rest[:n_cast], rest[n_cast:]
    outs, rest = rest[:n_out], rest[n_out:]
    w16_refs, rest = rest[:n_w16], rest[n_w16:]
    cast_dst, rest = rest[:n_cast], rest[n_cast:]
    h_ref, r_ref, x_buf, x_sem, o_sem = rest
    o_ref = outs[0]
    i, j = pl.program_id(0), pl.program_id(1)
    tm, d = x_buf.shape
    first = 1 if has_first else 0
    computed = i >= first

    def x_copy(tile):
        return pltpu.make_async_copy(x_hbm.at[pl.ds(tile * tm, tm), :], x_buf, x_sem)

    def init_copy(src):
        return pltpu.make_async_copy(src, o_ref, o_sem)

    if has_first:
        @pl.when((i == 0) & (j == 0))
        def _():
            init_copy(y0_hbm).start()
            init_copy(y0_hbm).wait()

    @pl.when((i == first) & (j == 0))
    def _():
        x_copy(first).start()

    @pl.when(computed & (j == 0))
    def _():
        residual = init_copy(x_hbm.at[pl.ds(i * tm, tm), :])
        residual.start()
        x_copy(i).wait()

        def norm_rows(rows):
            h_ref[rows, :] = _rmsnorm_rows(x_buf[rows, :], nw_ref[...]).astype(BF16)
        _for_row_chunks(tm, norm_rows)
        residual.wait()

    @pl.when(computed & (j == 1) & (i + 1 < pl.num_programs(0)))
    def _():
        x_copy(i + 1).start()

    def ffn_step():
        if own_cast:
            wg, wu, wd = (w_ref[...].astype(BF16) for w_ref in (wg_ref, wu_ref, wd_ref))
            for w16_ref, w in zip(w16_refs, (wg, wu, wd)):
                w16_ref[...] = w
        else:
            wg, wu, wd = wg_ref[...], wu_ref[...], wd_ref[...]
        h = h_ref[...]
        g = jnp.dot(h, wg, preferred_element_type=F32)
        u = jnp.dot(h, wu, preferred_element_type=F32)
        _run_casts(cast_src, cast_dst)
        a = (g * _sigmoid(g)) * (u * FFN_RESIDUAL_SCALE)
        o_ref[...] += jnp.dot(a.astype(BF16), wd, preferred_element_type=F32)

    if has_first:
        pl.when(computed)(ffn_step)
    else:
        ffn_step()

    if post is not None:
        @pl.when(j == pl.num_programs(1) - 1)
        def _():
            if post == "norm_copy":
                def copy_rows(rows):
                    outs[1][rows, :] = _rmsnorm_rows(o_ref[rows, :], pw_ref[...]).astype(BF16)
                _for_row_chunks(tm, copy_rows)
            else:
                def row_stats(rows):
                    y = o_ref[rows, :]
                    ms = jnp.mean(y * y, axis=-1, keepdims=True)
                    r_ref[rows, :] = jnp.broadcast_to(lax.rsqrt(ms + RMS_EPS),
                                                      (ROW_CHUNK, LANES))
                _for_row_chunks(tm, row_stats)

                def scale_rows(rows):
                    r = r_ref[rows, :]
                    o_ref[rows, :] = (o_ref[rows, :] * jnp.concatenate([r] * (d // LANES), axis=1)
                                      * pw_ref[...])
                _for_row_chunks(tm, scale_rows)


def _ffn(x, norm_w, w_gate, w_up, w_down, post_w, *, post, cast_weights=(), first_tile=None,
         tm=1024, tf=512):
    m, d = x.shape
    d_ff = w_gate.shape[1]
    assert m % tm == 0 and d_ff % tf == 0 and post in (None, "norm", "norm_copy")
    grid = (m // tm, d_ff // tf)
    assert grid[1] >= 2, "the x prefetch is started in step 1 of each row tile"
    has_first = first_tile is not None
    first = 1 if has_first else 0
    cast_specs, cast_shapes = _cast_specs(cast_weights, grid, first_row=first)
    row_spec = pl.BlockSpec((tm, d), lambda i, j: (i, 0))
    copy_specs = [row_spec] if post == "norm_copy" else []
    copy_shapes = [jax.ShapeDtypeStruct((m, d), BF16)] if post == "norm_copy" else []

    def col(i, j):
        return jnp.where(i < first, 0, j)

    windows = ([((tm, d), F32)] + [((tm, d), BF16)] * len(copy_specs) + [((d, tf), BF16)] * 3
               + [(s.block_shape, F32) for s in cast_specs]
               + [(s.block_shape, BF16) for s in cast_specs])
    scratch = [((tm, d), BF16), ((tm, LANES), F32), ((tm, d), F32)]
    return pl.pallas_call(
        functools.partial(_ffn_kernel, post=post, n_cast=len(cast_specs), own_cast=False,
                          has_first=has_first),
        grid=grid,
        in_specs=[
            pl.BlockSpec(memory_space=pl.ANY),
            pl.BlockSpec((1, d), lambda i, j: (0, 0)),
            pl.BlockSpec((d, tf), lambda i, j: (0, col(i, j))),
            pl.BlockSpec((d, tf), lambda i, j: (0, col(i, j))),
            pl.BlockSpec((tf, d), lambda i, j: (col(i, j), 0)),
            pl.BlockSpec((1, d), lambda i, j: (0, 0)),
        ] + [pl.BlockSpec(memory_space=pl.ANY)] * first + cast_specs,
        out_specs=[row_spec] + copy_specs + cast_specs,
        out_shape=[jax.ShapeDtypeStruct((m, d), F32)] + copy_shapes + cast_shapes,
        scratch_shapes=[pltpu.VMEM(s, t) for s, t in scratch] + FFN_DMA_SEMAPHORES,
        compiler_params=pltpu.CompilerParams(
            dimension_semantics=("arbitrary", "arbitrary"),
            vmem_limit_bytes=_vmem_limit(windows, scratch)),
        name="ffn",
    )(x, norm_w, w_gate, w_up, w_down, post_w, *([first_tile] * first), *cast_weights)


def _ffn_head(x, norm_w, w_gate, w_up, w_down, *, tm=1024, tf=256):
    m, d = x.shape
    d_ff = w_gate.shape[1]
    assert m % tm == 0 and d_ff % tf == 0
    grid = (1, d_ff // tf)
    specs = [pl.BlockSpec((d, tf), lambda i, j: (0, j)),
             pl.BlockSpec((d, tf), lambda i, j: (0, j)),
             pl.BlockSpec((tf, d), lambda i, j: (j, 0))]
    windows = [((d, tf), F32)] * 3 + [((d, tf), BF16)] * 3
    tile_spec = pl.BlockSpec((tm, d), lambda i, j: (0, 0), pipeline_mode=pl.Buffered(1))
    scratch = [((tm, d), BF16), ((tm, LANES), F32), ((tm, d), F32)]
    return pl.pallas_call(
        functools.partial(_ffn_kernel, post=None, n_cast=0, own_cast=True, has_first=False),
        grid=grid,
        in_specs=[pl.BlockSpec(memory_space=pl.ANY),
                  pl.BlockSpec((1, d), lambda i, j: (0, 0))] + specs
                 + [pl.BlockSpec((1, d), lambda i, j: (0, 0))],
        out_specs=[tile_spec] + specs,
        out_shape=[jax.ShapeDtypeStruct((tm, d), F32)]
                  + [jax.ShapeDtypeStruct(w.shape, BF16) for w in (w_gate, w_up, w_down)],
        scratch_shapes=[pltpu.VMEM(s, t) for s, t in scratch] + FFN_DMA_SEMAPHORES,
        compiler_params=pltpu.CompilerParams(
            dimension_semantics=("arbitrary", "arbitrary"),
            vmem_limit_bytes=_vmem_limit(windows, scratch + [((tm, d), F32)])),
        name="ffn_head",
    )(x, norm_w, w_gate, w_up, w_down, norm_w)


def _mixer_kernel(hn_ref, hnx_ref, wq_ref, wf_ref, wi_ref, wg_ref, wcb_ref, wcc_ref, wch_ref,
                  lb_ref, nw_ref, cw_ref, *rest, layer, group, n_cast):
    cast_src, (oa_ref, ob_ref, *cast_dst), (p_ref, st_ref, halo_ref) = (
        rest[:n_cast], rest[n_cast:2 * n_cast + 2], rest[2 * n_cast + 2:])
    _mixer_body(hn_ref, hnx_ref, (wq_ref, wf_ref, wi_ref, wg_ref, wcb_ref, wcc_ref, wch_ref),
                lb_ref, nw_ref, cw_ref, cast_src, oa_ref, ob_ref, cast_dst, p_ref, st_ref,
                halo_ref, layer=layer, group=group)


def _mixer_body(hn_ref, hnx_ref, w_refs, lb_ref, nw_ref, cw_ref, cast_src, oa_ref, ob_ref,
                cast_dst, p_ref, st_ref, halo_ref, *, layer, group):
    seq = hn_ref.shape[0]
    slab = group * CHUNK
    n_slabs = seq // slab
    SEC_Q, SEC_F, SEC_I, SEC_G, SEC_CB, SEC_CC, SEC_CH = range(N_SECTIONS)

    logits = lb_ref[...]
    e = jnp.exp(logits - jnp.max(logits, axis=0, keepdims=True))
    lb_all = jnp.sum(e[:layer + 1], axis=0, keepdims=True) / jnp.sum(e, axis=0, keepdims=True)
    nw_all = nw_ref[...]

    causal = (lax.broadcasted_iota(jnp.int32, (CHUNK, CHUNK), 0)
              >= lax.broadcasted_iota(jnp.int32, (CHUNK, CHUNK), 1))
    srow_i = lax.broadcasted_iota(jnp.int32, (CUMSUM_ROWS, CUMSUM_ROWS), 0)
    scol_i = lax.broadcasted_iota(jnp.int32, (CUMSUM_ROWS, CUMSUM_ROWS), 1)
    tril16 = ((srow_i >= scol_i) & (srow_i // CHUNK == scol_i // CHUNK)).astype(BF16)
    top_row = lax.broadcasted_iota(jnp.int32, (SUBLANES, LANES), 0)
    heads_in_step = range(PAIR)
    lanes = [slice(hh * HEAD_DIM, (hh + 1) * HEAD_DIM) for hh in heads_in_step]
    chunks = [slice(c * CHUNK, (c + 1) * CHUNK) for c in range(group)]

    def project_tasks(load_rows, slot):
        def task(s):
            p_ref[slot, s] = jnp.dot(load_rows(), w_refs[s][...], preferred_element_type=F32)
        return [functools.partial(task, s) for s in range(N_SECTIONS)]

    def mix_tasks(row0, slot):
        v = {}
        rows = pl.ds(row0, slab)

        def sec(s, hh):
            return p_ref[slot, s, :, lanes[hh]]

        def gates():
            v["hl"], v["k"] = [], []
            for hh in heads_in_step:
                lb = lb_all[:, lanes[hh]]
                f = lb + (1.0 - lb) * _sigmoid(sec(SEC_F, hh))
                log_f = jnp.log(f)
                v["k"].append(1.0 - f)
                hi = log_f.astype(BF16)
                lo = (log_f - hi.astype(F32)).astype(BF16)
                v["hl"].append(jnp.concatenate([hi, lo], axis=1))

        def cumsum():
            v["b"] = []
            for hh in heads_in_step:
                bb = jnp.concatenate(
                    [jnp.dot(tril16, v["hl"][hh][r0:r0 + CUMSUM_ROWS],
                             preferred_element_type=F32)
                     for r0 in range(0, slab, CUMSUM_ROWS)], axis=0)
                v["b"].append(bb[:, :HEAD_DIM] + bb[:, HEAD_DIM:])

        def decays():
            v["q_dec"], v["k_dec"], v["k_state"], v["decay"], v["v16"] = [], [], [], [], []
            for hh in heads_in_step:
                b = v["b"][hh]
                eb = jnp.exp(b)
                v["q_dec"].append((sec(SEC_Q, hh) * eb).astype(BF16))
                k_dec32 = v["k"][hh] * (1.0 / eb)
                v["k_dec"].append(k_dec32.astype(BF16))
                dec = [jnp.exp(b[ch.stop - 1:ch.stop, :]) for ch in chunks]
                v["decay"].append(dec)
                v["k_state"].append([(k_dec32[ch] * d).astype(BF16)
                                     for ch, d in zip(chunks, dec)])
                v["v16"].append(sec(SEC_I, hh).astype(BF16))

        def chunk_dots():
            v["scores"], v["update"] = [], []
            for hh in heads_in_step:
                v32 = sec(SEC_I, hh)
                v["scores"].append([lax.dot_general(v["q_dec"][hh][ch], v["k_dec"][hh][ch],
                                                    (((1,), (1,)), ((), ())),
                                                    preferred_element_type=F32)
                                    for ch in chunks])
                v["update"].append([jnp.dot(v32[ch].T.astype(BF16), v["k_state"][hh][c],
                                            preferred_element_type=F32)
                                    for c, ch in enumerate(chunks)])

        def read_out():
            v["o"] = []
            for hh in heads_in_step:
                o_intra = [jnp.dot(jnp.where(causal, v["scores"][hh][c], 0.0).astype(BF16),
                                   v["v16"][hh][ch], preferred_element_type=F32)
                           for c, ch in enumerate(chunks)]
                st_t = st_ref[hh]
                o_inter = []
                for c, ch in enumerate(chunks):
                    o_inter.append(lax.dot_general(v["q_dec"][hh][ch], st_t.astype(BF16),
                                                   (((1,), (1,)), ((), ())),
                                                   preferred_element_type=F32))
                    st_t = st_t * v["decay"][hh][c] + v["update"][hh][c]
                st_ref[hh] = st_t
                v["o"].append(jnp.concatenate([a + b for a, b in zip(o_intra, o_inter)], axis=0))

        def norm_gate():
            for hh in heads_in_step:
                o = v["o"][hh]
                gate = sec(SEC_G, hh)
                o = (o * lax.rsqrt(jnp.mean(o * o, axis=-1, keepdims=True) + RMS_EPS)
                     * nw_all[:, lanes[hh]])
                o = o * (gate * _sigmoid(gate))
                oa_ref[rows, lanes[hh]] = o.astype(oa_ref.dtype)

        def conv():
            for hh in heads_in_step:
                cw = cw_ref[:, lanes[hh]]
                u_in = sec(SEC_CC, hh) * sec(SEC_CH, hh)
                halo = halo_ref[hh]
                y = cw[CONV_K - 1:CONV_K, :] * u_in
                for tap in range(CONV_K - 1):
                    shift = CONV_K - 1 - tap
                    shifted = pltpu.roll(u_in, shift, axis=0)
                    top = shifted[:SUBLANES]
                    for r0 in range(shift):
                        hrow = SUBLANES - shift + r0
                        top = jnp.where(top_row == r0, halo[hrow:hrow + 1, :], top)
                    shifted = jnp.concatenate([top, shifted[SUBLANES:]], axis=0)
                    y = y + cw[tap:tap + 1, :] * shifted
                halo_ref[hh] = u_in[slab - SUBLANES:, :]
                ob_ref[rows, lanes[hh]] = (sec(SEC_CB, hh) * y).astype(ob_ref.dtype)

        return [gates, cumsum, decays, chunk_dots, read_out, norm_gate, conv]

    def run_interleaved(mix_list, proj_list):
        n = max(len(mix_list), len(proj_list))
        for t in range(n):
            for tasks in (proj_list, mix_list):
                for task in tasks[t * len(tasks) // n:(t + 1) * len(tasks) // n]:
                    task()

    @pl.when(pl.program_id(1) == 0)
    def _():
        for task in project_tasks(lambda: hn_ref[0:slab, :], 0):
            task()

    st_ref[...] = jnp.zeros_like(st_ref)
    halo_ref[...] = jnp.zeros_like(halo_ref)

    def slab_pair(i, carry):
        row0 = pl.multiple_of(i * (2 * slab), 2 * slab)
        run_interleaved(mix_tasks(row0, 0),
                        project_tasks(lambda: hn_ref[pl.ds(row0 + slab, slab), :], 1))
        run_interleaved(mix_tasks(row0 + slab, 1),
                        project_tasks(lambda: hn_ref[pl.ds(row0 + 2 * slab, slab), :], 0))
        return carry

    lax.fori_loop(0, n_slabs // 2 - 1, slab_pair, 0)
    row0 = (n_slabs - 2) * slab
    run_interleaved(mix_tasks(row0, 0),
                    project_tasks(lambda: hn_ref[row0 + slab:row0 + 2 * slab, :], 1))
    run_interleaved(mix_tasks(row0 + slab, 1), project_tasks(lambda: hnx_ref[...], 0))
    _run_casts(cast_src, cast_dst)


def _mixer(hn, w_in, lb_logits, out_norm_w, conv_w, *, layer, batch, heads, cast_weights=(),
           group=8):
    m, d = hn.shape
    seq = m // batch
    width = heads * HEAD_DIM
    pw = PAIR * HEAD_DIM
    n_pairs = heads // PAIR
    slab = group * CHUNK
    n_slabs = seq // slab
    assert heads % PAIR == 0 and w_in.shape == (d, N_SECTIONS * width)
    assert seq % slab == 0 and n_slabs % 2 == 0 and n_slabs >= 4 and CONV_K - 1 <= SUBLANES
    assert slab % CUMSUM_ROWS == 0 and CUMSUM_ROWS % CHUNK == 0

    def w_spec(section):
        return pl.BlockSpec((d, pw), lambda p, b, s=section: (0, s * n_pairs + p),
                            pipeline_mode=pl.Buffered(1))

    def vec_spec(rows):
        return pl.BlockSpec((rows, pw), lambda p, b: (0, p))

    grid = (n_pairs, batch)
    cast_specs, cast_shapes = _cast_specs(cast_weights, grid)
    out_spec = pl.BlockSpec((seq, pw), lambda p, b: (b, p))
    windows = ([((seq, d), BF16), ((slab, d), BF16)] + [((seq, pw), BF16)] * 2
               + [(s.block_shape, F32) for s in cast_specs]
               + [(s.block_shape, BF16) for s in cast_specs])
    slab_proj = ((N_SECTIONS, slab, pw), F32)
    resident = ([((d, pw), BF16)] * N_SECTIONS + [slab_proj] * 3
                + [((PAIR, HEAD_DIM, HEAD_DIM), F32), ((PAIR, SUBLANES, LANES), F32)])
    return pl.pallas_call(
        functools.partial(_mixer_kernel, layer=layer, group=group, n_cast=len(cast_specs)),
        grid=grid,
        in_specs=[pl.BlockSpec((seq, d), lambda p, b: (b, 0)),
                  pl.BlockSpec((slab, d), lambda p, b: (((b + 1) % batch) * n_slabs, 0))]
                 + [w_spec(s) for s in range(N_SECTIONS)]
                 + [vec_spec(lb_logits.shape[0]), vec_spec(1), vec_spec(CONV_K)]
                 + cast_specs,
        out_specs=[out_spec, out_spec] + cast_specs,
        out_shape=[jax.ShapeDtypeStruct((m, width), BF16)] * 2 + cast_shapes,
        scratch_shapes=[pltpu.VMEM((2,) + slab_proj[0], F32),
                        pltpu.VMEM((PAIR, HEAD_DIM, HEAD_DIM), F32),
                        pltpu.VMEM((PAIR, SUBLANES, LANES), F32)],
        compiler_params=pltpu.CompilerParams(
            dimension_semantics=("arbitrary", "arbitrary"),
            vmem_limit_bytes=_vmem_limit(windows, resident)),
        name="mixer",
    )(hn, hn, *([w_in] * N_SECTIONS), lb_logits, out_norm_w, conv_w, *cast_weights)


def _outproj_kernel(oa_ref, ob_ref, x_ref, wa_ref, wb_ref, o_ref):
    o_ref[...] = (x_ref[...]
                  + jnp.dot(oa_ref[...], wa_ref[...], preferred_element_type=F32)
                  + jnp.dot(ob_ref[...], wb_ref[...], preferred_element_type=F32))


def _out_proj(oa, ob, x, w_out, *, tm=512):
    m, width = oa.shape
    d_mix, d = w_out.shape
    assert d_mix == 2 * width and m % tm == 0
    return pl.pallas_call(
        _outproj_kernel,
        grid=(m // tm,),
        in_specs=[
            pl.BlockSpec((tm, width), lambda i: (i, 0)),
            pl.BlockSpec((tm, width), lambda i: (i, 0)),
            pl.BlockSpec((tm, d), lambda i: (i, 0)),
            pl.BlockSpec((width, d), lambda i: (0, 0)),
            pl.BlockSpec((width, d), lambda i: (1, 0)),
        ],
        out_specs=pl.BlockSpec((tm, d), lambda i: (i, 0)),
        out_shape=jax.ShapeDtypeStruct((m, d), F32),
        compiler_params=pltpu.CompilerParams(
            dimension_semantics=("parallel",),
            vmem_limit_bytes=_vmem_limit(
                [((tm, width), BF16)] * 2 + [((tm, d), F32)] * 2 + [((width, d), BF16)] * 2, [])),
        name="out_proj",
    )(oa, ob, x, w_out, w_out)


def kernel(x, ffn1_norm_w, ffn1_w_gate, ffn1_w_up, ffn1_w_down, mix_norm_w, w_in_mix,
           hgrn_lb_logits, hgrn_out_norm_w, conv_w, w_out_mix, ffn2_norm_w, ffn2_w_gate,
           ffn2_w_up, ffn2_w_down, final_norm_w):
    batch, seq, d = x.shape
    depth = ffn1_norm_w.shape[0]
    m = batch * seq
    heads = hgrn_out_norm_w.shape[1] // HEAD_DIM
    assert depth >= 1 and conv_w.shape[2] == heads * LANES
    final_w = final_norm_w.reshape(1, d)

    y = x.reshape(m, d)
    for l in range(depth):
        last = l == depth - 1
        y0, w1_gate, w1_up, w1_down = _ffn_head(
            y, ffn1_norm_w[l].reshape(1, d), ffn1_w_gate[l], ffn1_w_up[l], ffn1_w_down[l])
        y, hn, w_in, w2_gate, w2_up = _ffn(
            y, ffn1_norm_w[l].reshape(1, d), w1_gate, w1_up, w1_down,
            mix_norm_w[l].reshape(1, d), post="norm_copy", first_tile=y0,
            cast_weights=(w_in_mix[l], ffn2_w_gate[l], ffn2_w_up[l]))
        oa, ob, w2_down, w_out = _mixer(
            hn, w_in, hgrn_lb_logits, hgrn_out_norm_w[l].reshape(1, -1), conv_w[l],
            layer=l, batch=batch, heads=heads, cast_weights=(ffn2_w_down[l], w_out_mix[l]))
        y = _out_proj(oa, ob, y, w_out)
        (y,) = _ffn(y, ffn2_norm_w[l].reshape(1, d), w2_gate, w2_up, w2_down, final_w,
                    post="norm" if last else None)
    return y.reshape(batch, seq, d)
```

```python
import functools
import math

import jax
import jax.numpy as jnp
from jax import lax
from jax.experimental import pallas as pl
from jax.experimental.pallas import tpu as pltpu

LANES = 128
SUBLANES = 8
CHUNK = 64
HEAD_DIM = 128
PAIR = 2
N_SECTIONS = 7
CONV_K = 3
CUMSUM_ROWS = 256
RMS_EPS = 1e-6
FFN_RESIDUAL_SCALE = 0.5
MIB = 1024 * 1024
BF16_SUBLANES = 16
V7X_VMEM_BYTES = 64 * MIB
VMEM_RESERVED_BYTES = 4 * MIB
VMEM_COMPILER_TEMP_BYTES = 8 * MIB
ROW_CHUNK = 16
ROW_CHUNK_UNROLL = 8

F32 = jnp.float32
BF16 = jnp.bfloat16


def _rmsnorm_rows(x, w):
    return x * lax.rsqrt(jnp.mean(x * x, axis=-1, keepdims=True) + RMS_EPS) * w


def _for_row_chunks(n_rows, fn):
    def body(r, carry):
        fn(pl.ds(pl.multiple_of(r * ROW_CHUNK, ROW_CHUNK), ROW_CHUNK))
        return carry
    lax.fori_loop(0, n_rows // ROW_CHUNK, body, 0, unroll=ROW_CHUNK_UNROLL)


def _sigmoid(x):
    return 1.0 / (1.0 + jnp.exp(-x))


def _nbytes(shape, dtype):
    return math.prod(shape) * jnp.dtype(dtype).itemsize


def _vmem_limit(pipelined, resident):
    estimate = (2 * sum(_nbytes(s, t) for s, t in pipelined)
                + sum(_nbytes(s, t) for s, t in resident) + VMEM_COMPILER_TEMP_BYTES)
    return min(estimate, V7X_VMEM_BYTES - VMEM_RESERVED_BYTES)


def _cast_rows_per_block(rows, n_steps):
    rpb = BF16_SUBLANES
    while rows % rpb or rows // rpb > n_steps:
        rpb += BF16_SUBLANES
        assert rpb <= rows, "no row blocking of this weight fits the grid"
    return rpb


def _cast_specs(weights, grid, first_row=0):
    n_inner = math.prod(grid[1:])
    specs, shapes = [], []
    for w in weights:
        rows, cols = w.shape
        rpb = _cast_rows_per_block(rows, (grid[0] - first_row) * n_inner)
        last = rows // rpb - 1

        def index_map(i, *inner, last=last):
            step = (i - first_row) * n_inner + (inner[0] if inner else 0)
            return (jnp.clip(step, 0, last), 0)

        specs.append(pl.BlockSpec((rpb, cols), index_map))
        shapes.append(jax.ShapeDtypeStruct((rows, cols), BF16))
    return specs, shapes


def _run_casts(src_refs, dst_refs):
    for src, dst in zip(src_refs, dst_refs):
        dst[...] = src[...].astype(dst.dtype)


FFN_DMA_SEMAPHORES = [pltpu.SemaphoreType.DMA(())] * 2

def _ffn_kernel(x_hbm, nw_ref, wg_ref, wu_ref, wd_ref, pw_ref, *rest, post, n_cast, own_cast,
                has_first):
    rest = list(rest)
    y0_hbm = rest.pop(0) if has_first else None
    n_out = 2 if post == "norm_copy" else 1
    n_w16 = 3 if own_cast else 0
    cast_src, rest = rest[:n_cast], rest[n_cast:]
    outs, rest = rest[:n_out], rest[n_out:]
    w16_refs, rest = rest[:n_w16], rest[n_w16:]
    cast_dst, rest = rest[:n_cast], rest[n_cast:]
    h_ref, r_ref, x_buf, x_sem, o_sem = rest
    o_ref = outs[0]
    i, j = pl.program_id(0), pl.program_id(1)
    tm, d = x_buf.shape
    first = 1 if has_first else 0
    computed = i >= first

    def x_copy(tile):
        return pltpu.make_async_copy(x_hbm.at[pl.ds(tile * tm, tm), :], x_buf, x_sem)

    if has_first:
        @pl.when((i == 0) & (j == 0))
        def _():
            y0_copy = pltpu.make_async_copy(y0_hbm, o_ref, o_sem)
            y0_copy.start()
            y0_copy.wait()

    @pl.when((i == first) & (j == 0))
    def _():
        x_copy(first).start()

    @pl.when(computed & (j == 0))
    def _():
        x_copy(i).wait()

        def norm_rows(rows):
            x = x_buf[rows, :]
            h_ref[rows, :] = _rmsnorm_rows(x, nw_ref[...]).astype(BF16)
            o_ref[rows, :] = x
        _for_row_chunks(tm, norm_rows)

    @pl.when(computed & (j == pl.num_programs(1) // 2) & (i + 1 < pl.num_programs(0)))
    def _():
        x_copy(i + 1).start()

    def ffn_step():
        if own_cast:
            wg, wu, wd = (w_ref[...].astype(BF16) for w_ref in (wg_ref, wu_ref, wd_ref))
            for w16_ref, w in zip(w16_refs, (wg, wu, wd)):
                w16_ref[...] = w
        else:
            wg, wu, wd = wg_ref[...], wu_ref[...], wd_ref[...]
        h = h_ref[...]
        g = jnp.dot(h, wg, preferred_element_type=F32)
        u = jnp.dot(h, wu, preferred_element_type=F32)
        _run_casts(cast_src, cast_dst)
        a = (g * _sigmoid(g)) * (u * FFN_RESIDUAL_SCALE)
        o_ref[...] += jnp.dot(a.astype(BF16), wd, preferred_element_type=F32)

    if has_first:
        pl.when(computed)(ffn_step)
    else:
        ffn_step()

    if post is not None:
        @pl.when(j == pl.num_programs(1) - 1)
        def _():
            if post == "norm_copy":
                def copy_rows(rows):
                    outs[1][rows, :] = _rmsnorm_rows(o_ref[rows, :], pw_ref[...]).astype(BF16)
                _for_row_chunks(tm, copy_rows)
            else:
                def row_stats(rows):
                    y = o_ref[rows, :]
                    ms = jnp.mean(y * y, axis=-1, keepdims=True)
                    r_ref[rows, :] = jnp.broadcast_to(lax.rsqrt(ms + RMS_EPS),
                                                      (ROW_CHUNK, LANES))
                _for_row_chunks(tm, row_stats)

                def scale_rows(rows):
                    r = r_ref[rows, :]
                    o_ref[rows, :] = (o_ref[rows, :] * jnp.concatenate([r] * (d // LANES), axis=1)
                                      * pw_ref[...])
                _for_row_chunks(tm, scale_rows)


def _ffn(x, norm_w, w_gate, w_up, w_down, post_w, *, post, cast_weights=(), first_tile=None,
         tm=1024, tf=512):
    m, d = x.shape
    d_ff = w_gate.shape[1]
    assert m % tm == 0 and d_ff % tf == 0 and post in (None, "norm", "norm_copy")
    grid = (m // tm, d_ff // tf)
    assert grid[1] >= 2, "the x prefetch is started in a later step of each row tile"
    has_first = first_tile is not None
    first = 1 if has_first else 0
    cast_specs, cast_shapes = _cast_specs(cast_weights, grid, first_row=first)
    row_spec = pl.BlockSpec((tm, d), lambda i, j: (i, 0))
    copy_specs = [row_spec] if post == "norm_copy" else []
    copy_shapes = [jax.ShapeDtypeStruct((m, d), BF16)] if post == "norm_copy" else []

    def col(i, j):
        return jnp.where(i < first, 0, j)

    windows = ([((tm, d), F32)] + [((tm, d), BF16)] * len(copy_specs) + [((d, tf), BF16)] * 3
               + [(s.block_shape, F32) for s in cast_specs]
               + [(s.block_shape, BF16) for s in cast_specs])
    scratch = [((tm, d), BF16), ((tm, LANES), F32), ((tm, d), F32)]
    return pl.pallas_call(
        functools.partial(_ffn_kernel, post=post, n_cast=len(cast_specs), own_cast=False,
                          has_first=has_first),
        grid=grid,
        in_specs=[
            pl.BlockSpec(memory_space=pl.ANY),
            pl.BlockSpec((1, d), lambda i, j: (0, 0)),
            pl.BlockSpec((d, tf), lambda i, j: (0, col(i, j))),
            pl.BlockSpec((d, tf), lambda i, j: (0, col(i, j))),
            pl.BlockSpec((tf, d), lambda i, j: (col(i, j), 0)),
            pl.BlockSpec((1, d), lambda i, j: (0, 0)),
        ] + [pl.BlockSpec(memory_space=pl.ANY)] * first + cast_specs,
        out_specs=[row_spec] + copy_specs + cast_specs,
        out_shape=[jax.ShapeDtypeStruct((m, d), F32)] + copy_shapes + cast_shapes,
        scratch_shapes=[pltpu.VMEM(s, t) for s, t in scratch] + FFN_DMA_SEMAPHORES,
        compiler_params=pltpu.CompilerParams(
            dimension_semantics=("arbitrary", "arbitrary"),
            vmem_limit_bytes=_vmem_limit(windows, scratch)),
        name="ffn",
    )(x, norm_w, w_gate, w_up, w_down, post_w, *([first_tile] * first), *cast_weights)


def _ffn_head(x, norm_w, w_gate, w_up, w_down, *, tm=1024, tf=256):
    m, d = x.shape
    d_ff = w_gate.shape[1]
    assert m % tm == 0 and d_ff % tf == 0
    grid = (1, d_ff // tf)
    specs = [pl.BlockSpec((d, tf), lambda i, j: (0, j)),
             pl.BlockSpec((d, tf), lambda i, j: (0, j)),
             pl.BlockSpec((tf, d), lambda i, j: (j, 0))]
    windows = [((d, tf), F32)] * 3 + [((d, tf), BF16)] * 3
    tile_spec = pl.BlockSpec((tm, d), lambda i, j: (0, 0), pipeline_mode=pl.Buffered(1))
    scratch = [((tm, d), BF16), ((tm, LANES), F32), ((tm, d), F32)]
    return pl.pallas_call(
        functools.partial(_ffn_kernel, post=None, n_cast=0, own_cast=True, has_first=False),
        grid=grid,
        in_specs=[pl.BlockSpec(memory_space=pl.ANY),
                  pl.BlockSpec((1, d), lambda i, j: (0, 0))] + specs
                 + [pl.BlockSpec((1, d), lambda i, j: (0, 0))],
        out_specs=[tile_spec] + specs,
        out_shape=[jax.ShapeDtypeStruct((tm, d), F32)]
                  + [jax.ShapeDtypeStruct(w.shape, BF16) for w in (w_gate, w_up, w_down)],
        scratch_shapes=[pltpu.VMEM(s, t) for s, t in scratch] + FFN_DMA_SEMAPHORES,
        compiler_params=pltpu.CompilerParams(
            dimension_semantics=("arbitrary", "arbitrary"),
            vmem_limit_bytes=_vmem_limit(windows, scratch + [((tm, d), F32)])),
        name="ffn_head",
    )(x, norm_w, w_gate, w_up, w_down, norm_w)


def _mixer_kernel(hn_ref, hnx_ref, wq_ref, wf_ref, wi_ref, wg_ref, wcb_ref, wcc_ref, wch_ref,
                  lb_ref, nw_ref, cw_ref, *rest, layer, group, n_cast):
    cast_src, (oa_ref, ob_ref, *cast_dst), (p_ref, st_ref, halo_ref) = (
        rest[:n_cast], rest[n_cast:2 * n_cast + 2], rest[2 * n_cast + 2:])
    _mixer_body(hn_ref, hnx_ref, (wq_ref, wf_ref, wi_ref, wg_ref, wcb_ref, wcc_ref, wch_ref),
                lb_ref, nw_ref, cw_ref, cast_src, oa_ref, ob_ref, cast_dst, p_ref, st_ref,
                halo_ref, layer=layer, group=group)


def _mixer_body(hn_ref, hnx_ref, w_refs, lb_ref, nw_ref, cw_ref, cast_src, oa_ref, ob_ref,
                cast_dst, p_ref, st_ref, halo_ref, *, layer, group):
    seq = hn_ref.shape[0]
    slab = group * CHUNK
    n_slabs = seq // slab
    SEC_Q, SEC_F, SEC_I, SEC_G, SEC_CB, SEC_CC, SEC_CH = range(N_SECTIONS)

    logits = lb_ref[...]
    e = jnp.exp(logits - jnp.max(logits, axis=0, keepdims=True))
    lb_all = jnp.sum(e[:layer + 1], axis=0, keepdims=True) / jnp.sum(e, axis=0, keepdims=True)
    nw_all = nw_ref[...]

    causal = (lax.broadcasted_iota(jnp.int32, (CHUNK, CHUNK), 0)
              >= lax.broadcasted_iota(jnp.int32, (CHUNK, CHUNK), 1))
    srow_i = lax.broadcasted_iota(jnp.int32, (CUMSUM_ROWS, CUMSUM_ROWS), 0)
    scol_i = lax.broadcasted_iota(jnp.int32, (CUMSUM_ROWS, CUMSUM_ROWS), 1)
    tril16 = ((srow_i >= scol_i) & (srow_i // CHUNK == scol_i // CHUNK)).astype(BF16)
    top_row = lax.broadcasted_iota(jnp.int32, (SUBLANES, LANES), 0)
    heads_in_step = range(PAIR)
    lanes = [slice(hh * HEAD_DIM, (hh + 1) * HEAD_DIM) for hh in heads_in_step]
    chunks = [slice(c * CHUNK, (c + 1) * CHUNK) for c in range(group)]

    def project_tasks(load_rows, slot):
        def task(s):
            p_ref[slot, s] = jnp.dot(load_rows(), w_refs[s][...], preferred_element_type=F32)
        return [functools.partial(task, s) for s in range(N_SECTIONS)]

    def mix_tasks(row0, slot):
        v = {}
        rows = pl.ds(row0, slab)

        def sec(s, hh):
            return p_ref[slot, s, :, lanes[hh]]

        def gates():
            v["hl"], v["k"] = [], []
            for hh in heads_in_step:
                lb = lb_all[:, lanes[hh]]
                f = lb + (1.0 - lb) * _sigmoid(sec(SEC_F, hh))
                log_f = jnp.log(f)
                v["k"].append(1.0 - f)
                hi = log_f.astype(BF16)
                lo = (log_f - hi.astype(F32)).astype(BF16)
                v["hl"].append(jnp.concatenate([hi, lo], axis=1))

        def cumsum():
            v["b"] = []
            for hh in heads_in_step:
                bb = jnp.concatenate(
                    [jnp.dot(tril16, v["hl"][hh][r0:r0 + CUMSUM_ROWS],
                             preferred_element_type=F32)
                     for r0 in range(0, slab, CUMSUM_ROWS)], axis=0)
                v["b"].append(bb[:, :HEAD_DIM] + bb[:, HEAD_DIM:])

        def decays():
            v["q_dec"], v["k_dec"], v["k_state"], v["decay"], v["v16"] = [], [], [], [], []
            for hh in heads_in_step:
                b = v["b"][hh]
                eb = jnp.exp(b)
                v["q_dec"].append((sec(SEC_Q, hh) * eb).astype(BF16))
                k_dec32 = v["k"][hh] * (1.0 / eb)
                v["k_dec"].append(k_dec32.astype(BF16))
                dec = [jnp.exp(b[ch.stop - 1:ch.stop, :]) for ch in chunks]
                v["decay"].append(dec)
                v["k_state"].append([(k_dec32[ch] * d).astype(BF16)
                                     for ch, d in zip(chunks, dec)])
                v["v16"].append(sec(SEC_I, hh).astype(BF16))

        def chunk_dots():
            v["scores"], v["update"] = [], []
            for hh in heads_in_step:
                v32 = sec(SEC_I, hh)
                v["scores"].append([lax.dot_general(v["q_dec"][hh][ch], v["k_dec"][hh][ch],
                                                    (((1,), (1,)), ((), ())),
                                                    preferred_element_type=F32)
                                    for ch in chunks])
                v["update"].append([jnp.dot(v32[ch].T.astype(BF16), v["k_state"][hh][c],
                                            preferred_element_type=F32)
                                    for c, ch in enumerate(chunks)])

        def read_out():
            v["o"] = []
            for hh in heads_in_step:
                o_intra = [jnp.dot(jnp.where(causal, v["scores"][hh][c], 0.0).astype(BF16),
                                   v["v16"][hh][ch], preferred_element_type=F32)
                           for c, ch in enumerate(chunks)]
                st_t = st_ref[hh]
                o_inter = []
                for c, ch in enumerate(chunks):
                    o_inter.append(lax.dot_general(v["q_dec"][hh][ch], st_t.astype(BF16),
                                                   (((1,), (1,)), ((), ())),
                                                   preferred_element_type=F32))
                    st_t = st_t * v["decay"][hh][c] + v["update"][hh][c]
                st_ref[hh] = st_t
                v["o"].append(jnp.concatenate([a + b for a, b in zip(o_intra, o_inter)], axis=0))

        def norm_gate():
            for hh in heads_in_step:
                o = v["o"][hh]
                gate = sec(SEC_G, hh)
                o = (o * lax.rsqrt(jnp.mean(o * o, axis=-1, keepdims=True) + RMS_EPS)
                     * nw_all[:, lanes[hh]])
                o = o * (gate * _sigmoid(gate))
                oa_ref[rows, lanes[hh]] = o.astype(oa_ref.dtype)

        def conv():
            for hh in heads_in_step:
                cw = cw_ref[:, lanes[hh]]
                u_in = sec(SEC_CC, hh) * sec(SEC_CH, hh)
                halo = halo_ref[hh]
                y = cw[CONV_K - 1:CONV_K, :] * u_in
                for tap in range(CONV_K - 1):
                    shift = CONV_K - 1 - tap
                    shifted = pltpu.roll(u_in, shift, axis=0)
                    top = shifted[:SUBLANES]
                    for r0 in range(shift):
                        hrow = SUBLANES - shift + r0
                        top = jnp.where(top_row == r0, halo[hrow:hrow + 1, :], top)
                    shifted = jnp.concatenate([top, shifted[SUBLANES:]], axis=0)
                    y = y + cw[tap:tap + 1, :] * shifted
                halo_ref[hh] = u_in[slab - SUBLANES:, :]
                ob_ref[rows, lanes[hh]] = (sec(SEC_CB, hh) * y).astype(ob_ref.dtype)

        return [gates, cumsum, decays, chunk_dots, read_out, norm_gate, conv]

    def run_interleaved(mix_list, proj_list):
        n = max(len(mix_list), len(proj_list))
        for t in range(n):
            for tasks in (proj_list, mix_list):
                for task in tasks[t * len(tasks) // n:(t + 1) * len(tasks) // n]:
                    task()

    @pl.when(pl.program_id(1) == 0)
    def _():
        for task in project_tasks(lambda: hn_ref[0:slab, :], 0):
            task()

    st_ref[...] = jnp.zeros_like(st_ref)
    halo_ref[...] = jnp.zeros_like(halo_ref)

    def slab_pair(i, carry):
        row0 = pl.multiple_of(i * (2 * slab), 2 * slab)
        run_interleaved(mix_tasks(row0, 0),
                        project_tasks(lambda: hn_ref[pl.ds(row0 + slab, slab), :], 1))
        run_interleaved(mix_tasks(row0 + slab, 1),
                        project_tasks(lambda: hn_ref[pl.ds(row0 + 2 * slab, slab), :], 0))
        return carry

    lax.fori_loop(0, n_slabs // 2 - 1, slab_pair, 0)
    row0 = (n_slabs - 2) * slab
    run_interleaved(mix_tasks(row0, 0),
                    project_tasks(lambda: hn_ref[row0 + slab:row0 + 2 * slab, :], 1))
    run_interleaved(mix_tasks(row0 + slab, 1), project_tasks(lambda: hnx_ref[...], 0))
    _run_casts(cast_src, cast_dst)


def _mixer(hn, w_in, lb_logits, out_norm_w, conv_w, *, layer, batch, heads, cast_weights=(),
           group=8):
    m, d = hn.shape
    seq = m // batch
    width = heads * HEAD_DIM
    pw = PAIR * HEAD_DIM
    n_pairs = heads // PAIR
    slab = group * CHUNK
    n_slabs = seq // slab
    assert heads % PAIR == 0 and w_in.shape == (d, N_SECTIONS * width)
    assert seq % slab == 0 and n_slabs % 2 == 0 and n_slabs >= 4 and CONV_K - 1 <= SUBLANES
    assert slab % CUMSUM_ROWS == 0 and CUMSUM_ROWS % CHUNK == 0

    def w_spec(section):
        return pl.BlockSpec((d, pw), lambda p, b, s=section: (0, s * n_pairs + p),
                            pipeline_mode=pl.Buffered(1))

    def vec_spec(rows):
        return pl.BlockSpec((rows, pw), lambda p, b: (0, p))

    grid = (n_pairs, batch)
    cast_specs, cast_shapes = _cast_specs(cast_weights, grid)
    out_spec = pl.BlockSpec((seq, pw), lambda p, b: (b, p))
    windows = ([((seq, d), BF16), ((slab, d), BF16)] + [((seq, pw), BF16)] * 2
               + [(s.block_shape, F32) for s in cast_specs]
               + [(s.block_shape, BF16) for s in cast_specs])
    slab_proj = ((N_SECTIONS, slab, pw), F32)
    resident = ([((d, pw), BF16)] * N_SECTIONS + [slab_proj] * 3
                + [((PAIR, HEAD_DIM, HEAD_DIM), F32), ((PAIR, SUBLANES, LANES), F32)])
    return pl.pallas_call(
        functools.partial(_mixer_kernel, layer=layer, group=group, n_cast=len(cast_specs)),
        grid=grid,
        in_specs=[pl.BlockSpec((seq, d), lambda p, b: (b, 0)),
                  pl.BlockSpec((slab, d), lambda p, b: (((b + 1) % batch) * n_slabs, 0))]
                 + [w_spec(s) for s in range(N_SECTIONS)]
                 + [vec_spec(lb_logits.shape[0]), vec_spec(1), vec_spec(CONV_K)]
                 + cast_specs,
        out_specs=[out_spec, out_spec] + cast_specs,
        out_shape=[jax.ShapeDtypeStruct((m, width), BF16)] * 2 + cast_shapes,
        scratch_shapes=[pltpu.VMEM((2,) + slab_proj[0], F32),
                        pltpu.VMEM((PAIR, HEAD_DIM, HEAD_DIM), F32),
                        pltpu.VMEM((PAIR, SUBLANES, LANES), F32)],
        compiler_params=pltpu.CompilerParams(
            dimension_semantics=("arbitrary", "arbitrary"),
            vmem_limit_bytes=_vmem_limit(windows, resident)),
        name="mixer",
    )(hn, hn, *([w_in] * N_SECTIONS), lb_logits, out_norm_w, conv_w, *cast_weights)


def _outproj_kernel(oa_ref, ob_ref, x_ref, wa_ref, wb_ref, o_ref):
    o_ref[...] = (x_ref[...]
                  + jnp.dot(oa_ref[...], wa_ref[...], preferred_element_type=F32)
                  + jnp.dot(ob_ref[...], wb_ref[...], preferred_element_type=F32))


def _out_proj(oa, ob, x, w_out, *, tm=512):
    m, width = oa.shape
    d_mix, d = w_out.shape
    assert d_mix == 2 * width and m % tm == 0
    return pl.pallas_call(
        _outproj_kernel,
        grid=(m // tm,),
        in_specs=[
            pl.BlockSpec((tm, width), lambda i: (i, 0)),
            pl.BlockSpec((tm, width), lambda i: (i, 0)),
            pl.BlockSpec((tm, d), lambda i: (i, 0)),
            pl.BlockSpec((width, d), lambda i: (0, 0)),
            pl.BlockSpec((width, d), lambda i: (1, 0)),
        ],
        out_specs=pl.BlockSpec((tm, d), lambda i: (i, 0)),
        out_shape=jax.ShapeDtypeStruct((m, d), F32),
        compiler_params=pltpu.CompilerParams(
            dimension_semantics=("parallel",),
            vmem_limit_bytes=_vmem_limit(
                [((tm, width), BF16)] * 2 + [((tm, d), F32)] * 2 + [((width, d), BF16)] * 2, [])),
        name="out_proj",
    )(oa, ob, x, w_out, w_out)


def kernel(x, ffn1_norm_w, ffn1_w_gate, ffn1_w_up, ffn1_w_down, mix_norm_w, w_in_mix,
           hgrn_lb_logits, hgrn_out_norm_w, conv_w, w_out_mix, ffn2_norm_w, ffn2_w_gate,
           ffn2_w_up, ffn2_w_down, final_norm_w):
    batch, seq, d = x.shape
    depth = ffn1_norm_w.shape[0]
    m = batch * seq
    heads = hgrn_out_norm_w.shape[1] // HEAD_DIM
    assert depth >= 1 and conv_w.shape[2] == heads * LANES
    final_w = final_norm_w.reshape(1, d)

    y = x.reshape(m, d)
    for l in range(depth):
        last = l == depth - 1
        y0, w1_gate, w1_up, w1_down = _ffn_head(
            y, ffn1_norm_w[l].reshape(1, d), ffn1_w_gate[l], ffn1_w_up[l], ffn1_w_down[l])
        y, hn, w_in, w2_gate, w2_up = _ffn(
            y, ffn1_norm_w[l].reshape(1, d), w1_gate, w1_up, w1_down,
            mix_norm_w[l].reshape(1, d), post="norm_copy", first_tile=y0,
            cast_weights=(w_in_mix[l], ffn2_w_gate[l], ffn2_w_up[l]))
        oa, ob, w2_down, w_out = _mixer(
            hn, w_in, hgrn_lb_logits, hgrn_out_norm_w[l].reshape(1, -1), conv_w[l],
            layer=l, batch=batch, heads=heads, cast_weights=(ffn2_w_down[l], w_out_mix[l]))
        y = _out_proj(oa, ob, y, w_out)
        (y,) = _ffn(y, ffn2_norm_w[l].reshape(1, d), w2_gate, w2_up, w2_down, final_w,
                    post="norm" if last else None)
    return y.reshape(batch, seq, d)
```

```python
import functools
import math

import jax
import jax.numpy as jnp
from jax import lax
from jax.experimental import pallas as pl
from jax.experimental.pallas import tpu as pltpu

LANES = 128
SUBLANES = 8
CHUNK = 64
HEAD_DIM = 128
PAIR = 2
N_SECTIONS = 7
CONV_K = 3
CUMSUM_ROWS = 256
RMS_EPS = 1e-6
FFN_RESIDUAL_SCALE = 0.5
MIB = 1024 * 1024
BF16_SUBLANES = 16
V7X_VMEM_BYTES = 64 * MIB
VMEM_RESERVED_BYTES = 4 * MIB
VMEM_COMPILER_TEMP_BYTES = 8 * MIB
ROW_CHUNK = 16
ROW_CHUNK_UNROLL = 16

F32 = jnp.float32
BF16 = jnp.bfloat16


def _rmsnorm_rows(x, w):
    return x * lax.rsqrt(jnp.mean(x * x, axis=-1, keepdims=True) + RMS_EPS) * w


def _for_row_chunks(n_rows, fn):
    def body(r, carry):
        fn(pl.ds(pl.multiple_of(r * ROW_CHUNK, ROW_CHUNK), ROW_CHUNK))
        return carry
    lax.fori_loop(0, n_rows // ROW_CHUNK, body, 0, unroll=ROW_CHUNK_UNROLL)


def _sigmoid(x):
    return 1.0 / (1.0 + jnp.exp(-x))


def _nbytes(shape, dtype):
    return math.prod(shape) * jnp.dtype(dtype).itemsize


def _vmem_limit(pipelined, resident):
    estimate = (2 * sum(_nbytes(s, t) for s, t in pipelined)
                + sum(_nbytes(s, t) for s, t in resident) + VMEM_COMPILER_TEMP_BYTES)
    return min(estimate, V7X_VMEM_BYTES - VMEM_RESERVED_BYTES)


def _cast_rows_per_block(rows, n_steps):
    rpb = BF16_SUBLANES
    while rows % rpb or rows // rpb > n_steps:
        rpb += BF16_SUBLANES
        assert rpb <= rows, "no row blocking of this weight fits the grid"
    return rpb


def _cast_specs(weights, grid, first_row=0):
    n_inner = math.prod(grid[1:])
    specs, shapes = [], []
    for w in weights:
        rows, cols = w.shape
        rpb = _cast_rows_per_block(rows, (grid[0] - first_row) * n_inner)
        last = rows // rpb - 1

        def index_map(i, *inner, last=last):
            step = (i - first_row) * n_inner + (inner[0] if inner else 0)
            return (jnp.clip(step, 0, last), 0)

        specs.append(pl.BlockSpec((rpb, cols), index_map))
        shapes.append(jax.ShapeDtypeStruct((rows, cols), BF16))
    return specs, shapes


def _run_casts(src_refs, dst_refs):
    for src, dst in zip(src_refs, dst_refs):
        dst[...] = src[...].astype(dst.dtype)


FFN_DMA_SEMAPHORES = [pltpu.SemaphoreType.DMA(())] * 2

def _ffn_kernel(x_hbm, nw_ref, wg_ref, wu_ref, wd_ref, pw_ref, *rest, post, n_cast, own_cast,
                has_first):
    rest = list(rest)
    y0_hbm = rest.pop(0) if has_first else None
    n_out = 2 if post == "norm_copy" else 1
    n_w16 = 3 if own_cast else 0
    cast_src, rest = rest[:n_cast], rest[n_cast:]
    outs, rest = rest[:n_out], rest[n_out:]
    w16_refs, rest = rest[:n_w16], rest[n_w16:]
    cast_dst, rest = rest[:n_cast], rest[n_cast:]
    h_ref, r_ref, x_buf, x_sem, o_sem = rest
    o_ref = outs[0]
    i, j = pl.program_id(0), pl.program_id(1)
    tm, d = x_buf.shape
    first = 1 if has_first else 0
    computed = i >= first

    def x_copy(tile):
        return pltpu.make_async_copy(x_hbm.at[pl.ds(tile * tm, tm), :], x_buf, x_sem)

    if has_first:
        y0_copy = pltpu.make_async_copy(y0_hbm, o_ref, o_sem)

        @pl.when((i == 0) & (j == 0))
        def _():
            y0_copy.start()

        @pl.when((i == 0) & (j == 1))
        def _():
            y0_copy.wait()

    @pl.when((i == 0) & (j == 0))
    def _():
        x_copy(first).start()

    @pl.when(computed & (j == 0))
    def _():
        x_copy(i).wait()

        def norm_rows(rows):
            x = x_buf[rows, :]
            h_ref[rows, :] = _rmsnorm_rows(x, nw_ref[...]).astype(BF16)
            o_ref[rows, :] = x
        _for_row_chunks(tm, norm_rows)

    @pl.when(computed & (j == pl.num_programs(1) // 2) & (i + 1 < pl.num_programs(0)))
    def _():
        x_copy(i + 1).start()

    def ffn_step():
        if own_cast:
            wg, wu, wd = (w_ref[...].astype(BF16) for w_ref in (wg_ref, wu_ref, wd_ref))
            for w16_ref, w in zip(w16_refs, (wg, wu, wd)):
                w16_ref[...] = w
        else:
            wg, wu, wd = wg_ref[...], wu_ref[...], wd_ref[...]
        h = h_ref[...]
        g = jnp.dot(h, wg, preferred_element_type=F32)
        u = jnp.dot(h, wu, preferred_element_type=F32)
        _run_casts(cast_src, cast_dst)
        a = (g * _sigmoid(g)) * (u * FFN_RESIDUAL_SCALE)
        o_ref[...] += jnp.dot(a.astype(BF16), wd, preferred_element_type=F32)

    if has_first:
        pl.when(computed)(ffn_step)
    else:
        ffn_step()

    if post is not None:
        @pl.when(j == pl.num_programs(1) - 1)
        def _():
            if post == "norm_copy":
                def copy_rows(rows):
                    outs[1][rows, :] = _rmsnorm_rows(o_ref[rows, :], pw_ref[...]).astype(BF16)
                _for_row_chunks(tm, copy_rows)
            else:
                def row_stats(rows):
                    y = o_ref[rows, :]
                    ms = jnp.mean(y * y, axis=-1, keepdims=True)
                    r_ref[rows, :] = jnp.broadcast_to(lax.rsqrt(ms + RMS_EPS),
                                                      (ROW_CHUNK, LANES))
                _for_row_chunks(tm, row_stats)

                def scale_rows(rows):
                    r = r_ref[rows, :]
                    o_ref[rows, :] = (o_ref[rows, :] * jnp.concatenate([r] * (d // LANES), axis=1)
                                      * pw_ref[...])
                _for_row_chunks(tm, scale_rows)


def _ffn(x, norm_w, w_gate, w_up, w_down, post_w, *, post, cast_weights=(), first_tile=None,
         tm=1024, tf=512):
    m, d = x.shape
    d_ff = w_gate.shape[1]
    assert m % tm == 0 and d_ff % tf == 0 and post in (None, "norm", "norm_copy")
    grid = (m // tm, d_ff // tf)
    assert grid[1] >= 2, "the x prefetch is started in a later step of each row tile"
    has_first = first_tile is not None
    first = 1 if has_first else 0
    cast_specs, cast_shapes = _cast_specs(cast_weights, grid, first_row=first)
    row_spec = pl.BlockSpec((tm, d), lambda i, j: (i, 0))
    copy_specs = [row_spec] if post == "norm_copy" else []
    copy_shapes = [jax.ShapeDtypeStruct((m, d), BF16)] if post == "norm_copy" else []

    def col(i, j):
        return jnp.where(i < first, 0, j)

    windows = ([((tm, d), F32)] + [((tm, d), BF16)] * len(copy_specs) + [((d, tf), BF16)] * 3
               + [(s.block_shape, F32) for s in cast_specs]
               + [(s.block_shape, BF16) for s in cast_specs])
    scratch = [((tm, d), BF16), ((tm, LANES), F32), ((tm, d), F32)]
    return pl.pallas_call(
        functools.partial(_ffn_kernel, post=post, n_cast=len(cast_specs), own_cast=False,
                          has_first=has_first),
        grid=grid,
        in_specs=[
            pl.BlockSpec(memory_space=pl.ANY),
            pl.BlockSpec((1, d), lambda i, j: (0, 0)),
            pl.BlockSpec((d, tf), lambda i, j: (0, col(i, j))),
            pl.BlockSpec((d, tf), lambda i, j: (0, col(i, j))),
            pl.BlockSpec((tf, d), lambda i, j: (col(i, j), 0)),
            pl.BlockSpec((1, d), lambda i, j: (0, 0)),
        ] + [pl.BlockSpec(memory_space=pl.ANY)] * first + cast_specs,
        out_specs=[row_spec] + copy_specs + cast_specs,
        out_shape=[jax.ShapeDtypeStruct((m, d), F32)] + copy_shapes + cast_shapes,
        scratch_shapes=[pltpu.VMEM(s, t) for s, t in scratch] + FFN_DMA_SEMAPHORES,
        compiler_params=pltpu.CompilerParams(
            dimension_semantics=("arbitrary", "arbitrary"),
            vmem_limit_bytes=_vmem_limit(windows, scratch)),
        name="ffn",
    )(x, norm_w, w_gate, w_up, w_down, post_w, *([first_tile] * first), *cast_weights)


def _ffn_head(x, norm_w, w_gate, w_up, w_down, *, tm=1024, tf=256):
    m, d = x.shape
    d_ff = w_gate.shape[1]
    assert m % tm == 0 and d_ff % tf == 0
    grid = (1, d_ff // tf)
    specs = [pl.BlockSpec((d, tf), lambda i, j: (0, j)),
             pl.BlockSpec((d, tf), lambda i, j: (0, j)),
             pl.BlockSpec((tf, d), lambda i, j: (j, 0))]
    windows = [((d, tf), F32)] * 3 + [((d, tf), BF16)] * 3
    tile_spec = pl.BlockSpec((tm, d), lambda i, j: (0, 0), pipeline_mode=pl.Buffered(1))
    scratch = [((tm, d), BF16), ((tm, LANES), F32), ((tm, d), F32)]
    return pl.pallas_call(
        functools.partial(_ffn_kernel, post=None, n_cast=0, own_cast=True, has_first=False),
        grid=grid,
        in_specs=[pl.BlockSpec(memory_space=pl.ANY),
                  pl.BlockSpec((1, d), lambda i, j: (0, 0))] + specs
                 + [pl.BlockSpec((1, d), lambda i, j: (0, 0))],
        out_specs=[tile_spec] + specs,
        out_shape=[jax.ShapeDtypeStruct((tm, d), F32)]
                  + [jax.ShapeDtypeStruct(w.shape, BF16) for w in (w_gate, w_up, w_down)],
        scratch_shapes=[pltpu.VMEM(s, t) for s, t in scratch] + FFN_DMA_SEMAPHORES,
        compiler_params=pltpu.CompilerParams(
            dimension_semantics=("arbitrary", "arbitrary"),
            vmem_limit_bytes=_vmem_limit(windows, scratch + [((tm, d), F32)])),
        name="ffn_head",
    )(x, norm_w, w_gate, w_up, w_down, norm_w)


def _mixer_kernel(hn_ref, hnx_ref, wq_ref, wf_ref, wi_ref, wg_ref, wcb_ref, wcc_ref, wch_ref,
                  lb_ref, nw_ref, cw_ref, *rest, layer, group, n_cast):
    cast_src, (oa_ref, ob_ref, *cast_dst), (p_ref, st_ref, halo_ref) = (
        rest[:n_cast], rest[n_cast:2 * n_cast + 2], rest[2 * n_cast + 2:])
    _mixer_body(hn_ref, hnx_ref, (wq_ref, wf_ref, wi_ref, wg_ref, wcb_ref, wcc_ref, wch_ref),
                lb_ref, nw_ref, cw_ref, cast_src, oa_ref, ob_ref, cast_dst, p_ref, st_ref,
                halo_ref, layer=layer, group=group)


def _mixer_body(hn_ref, hnx_ref, w_refs, lb_ref, nw_ref, cw_ref, cast_src, oa_ref, ob_ref,
                cast_dst, p_ref, st_ref, halo_ref, *, layer, group):
    seq = hn_ref.shape[0]
    slab = group * CHUNK
    n_slabs = seq // slab
    SEC_Q, SEC_F, SEC_I, SEC_G, SEC_CB, SEC_CC, SEC_CH = range(N_SECTIONS)

    logits = lb_ref[...]
    e = jnp.exp(logits - jnp.max(logits, axis=0, keepdims=True))
    lb_all = jnp.sum(e[:layer + 1], axis=0, keepdims=True) / jnp.sum(e, axis=0, keepdims=True)
    nw_all = nw_ref[...]

    causal = (lax.broadcasted_iota(jnp.int32, (CHUNK, CHUNK), 0)
              >= lax.broadcasted_iota(jnp.int32, (CHUNK, CHUNK), 1))
    srow_i = lax.broadcasted_iota(jnp.int32, (CUMSUM_ROWS, CUMSUM_ROWS), 0)
    scol_i = lax.broadcasted_iota(jnp.int32, (CUMSUM_ROWS, CUMSUM_ROWS), 1)
    tril16 = ((srow_i >= scol_i) & (srow_i // CHUNK == scol_i // CHUNK)).astype(BF16)
    top_row = lax.broadcasted_iota(jnp.int32, (SUBLANES, LANES), 0)
    heads_in_step = range(PAIR)
    lanes = [slice(hh * HEAD_DIM, (hh + 1) * HEAD_DIM) for hh in heads_in_step]
    chunks = [slice(c * CHUNK, (c + 1) * CHUNK) for c in range(group)]

    def project_tasks(load_rows, slot):
        def task(s):
            p_ref[slot, s] = jnp.dot(load_rows(), w_refs[s][...], preferred_element_type=F32)
        return [functools.partial(task, s) for s in range(N_SECTIONS)]

    def mix_tasks(row0, slot):
        v = {}
        rows = pl.ds(row0, slab)

        def sec(s, hh):
            return p_ref[slot, s, :, lanes[hh]]

        def gates():
            v["hl"], v["k"] = [], []
            for hh in heads_in_step:
                lb = lb_all[:, lanes[hh]]
                f = lb + (1.0 - lb) * _sigmoid(sec(SEC_F, hh))
                log_f = jnp.log(f)
                v["k"].append(1.0 - f)
                hi = log_f.astype(BF16)
                lo = (log_f - hi.astype(F32)).astype(BF16)
                v["hl"].append(jnp.concatenate([hi, lo], axis=1))

        def cumsum():
            v["b"] = []
            for hh in heads_in_step:
                bb = jnp.concatenate(
                    [jnp.dot(tril16, v["hl"][hh][r0:r0 + CUMSUM_ROWS],
                             preferred_element_type=F32)
                     for r0 in range(0, slab, CUMSUM_ROWS)], axis=0)
                v["b"].append(bb[:, :HEAD_DIM] + bb[:, HEAD_DIM:])

        def decays():
            v["q_dec"], v["k_dec"], v["k_state"], v["decay"], v["v16"] = [], [], [], [], []
            for hh in heads_in_step:
                b = v["b"][hh]
                eb = jnp.exp(b)
                v["q_dec"].append((sec(SEC_Q, hh) * eb).astype(BF16))
                k_dec32 = v["k"][hh] * (1.0 / eb)
                v["k_dec"].append(k_dec32.astype(BF16))
                dec = [jnp.exp(b[ch.stop - 1:ch.stop, :]) for ch in chunks]
                v["decay"].append(dec)
                v["k_state"].append([(k_dec32[ch] * d).astype(BF16)
                                     for ch, d in zip(chunks, dec)])
                v["v16"].append(sec(SEC_I, hh).astype(BF16))

        def chunk_dots():
            v["scores"], v["update"] = [], []
            for hh in heads_in_step:
                v32 = sec(SEC_I, hh)
                v["scores"].append([lax.dot_general(v["q_dec"][hh][ch], v["k_dec"][hh][ch],
                                                    (((1,), (1,)), ((), ())),
                                                    preferred_element_type=F32)
                                    for ch in chunks])
                v["update"].append([jnp.dot(v32[ch].T.astype(BF16), v["k_state"][hh][c],
                                            preferred_element_type=F32)
                                    for c, ch in enumerate(chunks)])

        def read_out():
            v["o"] = []
            for hh in heads_in_step:
                o_intra = [jnp.dot(jnp.where(causal, v["scores"][hh][c], 0.0).astype(BF16),
                                   v["v16"][hh][ch], preferred_element_type=F32)
                           for c, ch in enumerate(chunks)]
                st_t = st_ref[hh]
                o_inter = []
                for c, ch in enumerate(chunks):
                    o_inter.append(lax.dot_general(v["q_dec"][hh][ch], st_t.astype(BF16),
                                                   (((1,), (1,)), ((), ())),
                                                   preferred_element_type=F32))
                    st_t = st_t * v["decay"][hh][c] + v["update"][hh][c]
                st_ref[hh] = st_t
                v["o"].append(jnp.concatenate([a + b for a, b in zip(o_intra, o_inter)], axis=0))

        def norm_gate():
            for hh in heads_in_step:
                o = v["o"][hh]
                gate = sec(SEC_G, hh)
                o = (o * lax.rsqrt(jnp.mean(o * o, axis=-1, keepdims=True) + RMS_EPS)
                     * nw_all[:, lanes[hh]])
                o = o * (gate * _sigmoid(gate))
                oa_ref[rows, lanes[hh]] = o.astype(oa_ref.dtype)

        def conv():
            for hh in heads_in_step:
                cw = cw_ref[:, lanes[hh]]
                u_in = sec(SEC_CC, hh) * sec(SEC_CH, hh)
                halo = halo_ref[hh]
                y = cw[CONV_K - 1:CONV_K, :] * u_in
                for tap in range(CONV_K - 1):
                    shift = CONV_K - 1 - tap
                    shifted = pltpu.roll(u_in, shift, axis=0)
                    top = shifted[:SUBLANES]
                    for r0 in range(shift):
                        hrow = SUBLANES - shift + r0
                        top = jnp.where(top_row == r0, halo[hrow:hrow + 1, :], top)
                    shifted = jnp.concatenate([top, shifted[SUBLANES:]], axis=0)
                    y = y + cw[tap:tap + 1, :] * shifted
                halo_ref[hh] = u_in[slab - SUBLANES:, :]
                ob_ref[rows, lanes[hh]] = (sec(SEC_CB, hh) * y).astype(ob_ref.dtype)

        return [gates, cumsum, decays, chunk_dots, read_out, norm_gate, conv]

    def run_interleaved(mix_list, proj_list):
        n = max(len(mix_list), len(proj_list))
        for t in range(n):
            for tasks in (proj_list, mix_list):
                for task in tasks[t * len(tasks) // n:(t + 1) * len(tasks) // n]:
                    task()

    @pl.when(pl.program_id(1) == 0)
    def _():
        for task in project_tasks(lambda: hn_ref[0:slab, :], 0):
            task()

    st_ref[...] = jnp.zeros_like(st_ref)
    halo_ref[...] = jnp.zeros_like(halo_ref)

    def slab_pair(i, carry):
        row0 = pl.multiple_of(i * (2 * slab), 2 * slab)
        run_interleaved(mix_tasks(row0, 0),
                        project_tasks(lambda: hn_ref[pl.ds(row0 + slab, slab), :], 1))
        run_interleaved(mix_tasks(row0 + slab, 1),
                        project_tasks(lambda: hn_ref[pl.ds(row0 + 2 * slab, slab), :], 0))
        return carry

    lax.fori_loop(0, n_slabs // 2 - 1, slab_pair, 0)
    row0 = (n_slabs - 2) * slab
    run_interleaved(mix_tasks(row0, 0),
                    project_tasks(lambda: hn_ref[row0 + slab:row0 + 2 * slab, :], 1))
    run_interleaved(mix_tasks(row0 + slab, 1), project_tasks(lambda: hnx_ref[...], 0))
    _run_casts(cast_src, cast_dst)


def _mixer(hn, w_in, lb_logits, out_norm_w, conv_w, *, layer, batch, heads, cast_weights=(),
           group=8):
    m, d = hn.shape
    seq = m // batch
    width = heads * HEAD_DIM
    pw = PAIR * HEAD_DIM
    n_pairs = heads // PAIR
    slab = group * CHUNK
    n_slabs = seq // slab
    assert heads % PAIR == 0 and w_in.shape == (d, N_SECTIONS * width)
    assert seq % slab == 0 and n_slabs % 2 == 0 and n_slabs >= 4 and CONV_K - 1 <= SUBLANES
    assert slab % CUMSUM_ROWS == 0 and CUMSUM_ROWS % CHUNK == 0

    def w_spec(section):
        return pl.BlockSpec((d, pw), lambda p, b, s=section: (0, s * n_pairs + p),
                            pipeline_mode=pl.Buffered(1))

    def vec_spec(rows):
        return pl.BlockSpec((rows, pw), lambda p, b: (0, p))

    grid = (n_pairs, batch)
    cast_specs, cast_shapes = _cast_specs(cast_weights, grid)
    out_spec = pl.BlockSpec((seq, pw), lambda p, b: (b, p))
    windows = ([((seq, d), BF16), ((slab, d), BF16)] + [((seq, pw), BF16)] * 2
               + [(s.block_shape, F32) for s in cast_specs]
               + [(s.block_shape, BF16) for s in cast_specs])
    slab_proj = ((N_SECTIONS, slab, pw), F32)
    resident = ([((d, pw), BF16)] * N_SECTIONS + [slab_proj] * 3
                + [((PAIR, HEAD_DIM, HEAD_DIM), F32), ((PAIR, SUBLANES, LANES), F32)])
    return pl.pallas_call(
        functools.partial(_mixer_kernel, layer=layer, group=group, n_cast=len(cast_specs)),
        grid=grid,
        in_specs=[pl.BlockSpec((seq, d), lambda p, b: (b, 0)),
                  pl.BlockSpec((slab, d), lambda p, b: (((b + 1) % batch) * n_slabs, 0))]
                 + [w_spec(s) for s in range(N_SECTIONS)]
                 + [vec_spec(lb_logits.shape[0]), vec_spec(1), vec_spec(CONV_K)]
                 + cast_specs,
        out_specs=[out_spec, out_spec] + cast_specs,
        out_shape=[jax.ShapeDtypeStruct((m, width), BF16)] * 2 + cast_shapes,
        scratch_shapes=[pltpu.VMEM((2,) + slab_proj[0], F32),
                        pltpu.VMEM((PAIR, HEAD_DIM, HEAD_DIM), F32),
                        pltpu.VMEM((PAIR, SUBLANES, LANES), F32)],
        compiler_params=pltpu.CompilerParams(
            dimension_semantics=("arbitrary", "arbitrary"),
            vmem_limit_bytes=_vmem_limit(windows, resident)),
        name="mixer",
    )(hn, hn, *([w_in] * N_SECTIONS), lb_logits, out_norm_w, conv_w, *cast_weights)


def _outproj_kernel(oa_ref, ob_ref, x_ref, wa_ref, wb_ref, o_ref):
    o_ref[...] = (x_ref[...]
                  + jnp.dot(oa_ref[...], wa_ref[...], preferred_element_type=F32)
                  + jnp.dot(ob_ref[...], wb_ref[...], preferred_element_type=F32))


def _out_proj(oa, ob, x, w_out, *, tm=512):
    m, width = oa.shape
    d_mix, d = w_out.shape
    assert d_mix == 2 * width and m % tm == 0
    return pl.pallas_call(
        _outproj_kernel,
        grid=(m // tm,),
        in_specs=[
            pl.BlockSpec((tm, width), lambda i: (i, 0)),
            pl.BlockSpec((tm, width), lambda i: (i, 0)),
            pl.BlockSpec((tm, d), lambda i: (i, 0)),
            pl.BlockSpec((width, d), lambda i: (0, 0)),
            pl.BlockSpec((width, d), lambda i: (1, 0)),
        ],
        out_specs=pl.BlockSpec((tm, d), lambda i: (i, 0)),
        out_shape=jax.ShapeDtypeStruct((m, d), F32),
        compiler_params=pltpu.CompilerParams(
            dimension_semantics=("parallel",),
            vmem_limit_bytes=_vmem_limit(
                [((tm, width), BF16)] * 2 + [((tm, d), F32)] * 2 + [((width, d), BF16)] * 2, [])),
        name="out_proj",
    )(oa, ob, x, w_out, w_out)


def kernel(x, ffn1_norm_w, ffn1_w_gate, ffn1_w_up, ffn1_w_down, mix_norm_w, w_in_mix,
           hgrn_lb_logits, hgrn_out_norm_w, conv_w, w_out_mix, ffn2_norm_w, ffn2_w_gate,
           ffn2_w_up, ffn2_w_down, final_norm_w):
    batch, seq, d = x.shape
    depth = ffn1_norm_w.shape[0]
    m = batch * seq
    heads = hgrn_out_norm_w.shape[1] // HEAD_DIM
    assert depth >= 1 and conv_w.shape[2] == heads * LANES
    final_w = final_norm_w.reshape(1, d)

    y = x.reshape(m, d)
    for l in range(depth):
        last = l == depth - 1
        y0, w1_gate, w1_up, w1_down = _ffn_head(
            y, ffn1_norm_w[l].reshape(1, d), ffn1_w_gate[l], ffn1_w_up[l], ffn1_w_down[l])
        y, hn, w_in, w2_gate, w2_up = _ffn(
            y, ffn1_norm_w[l].reshape(1, d), w1_gate, w1_up, w1_down,
            mix_norm_w[l].reshape(1, d), post="norm_copy", first_tile=y0,
            cast_weights=(w_in_mix[l], ffn2_w_gate[l], ffn2_w_up[l]))
        oa, ob, w2_down, w_out = _mixer(
            hn, w_in, hgrn_lb_logits, hgrn_out_norm_w[l].reshape(1, -1), conv_w[l],
            layer=l, batch=batch, heads=heads, cast_weights=(ffn2_w_down[l], w_out_mix[l]))
        y = _out_proj(oa, ob, y, w_out)
        (y,) = _ffn(y, ffn2_norm_w[l].reshape(1, d), w2_gate, w2_up, w2_down, final_w,
                    post="norm" if last else None)
    return y.reshape(batch, seq, d)
```

```python
import functools
import math

import jax
import jax.numpy as jnp
from jax import lax
from jax.experimental import pallas as pl
from jax.experimental.pallas import tpu as pltpu

LANES = 128
SUBLANES = 8
CHUNK = 64
HEAD_DIM = 128
PAIR = 2
N_SECTIONS = 7
CONV_K = 3
CUMSUM_ROWS = 256
RMS_EPS = 1e-6
FFN_RESIDUAL_SCALE = 0.5
MIB = 1024 * 1024
BF16_SUBLANES = 16
V7X_VMEM_BYTES = 64 * MIB
VMEM_RESERVED_BYTES = 4 * MIB
VMEM_COMPILER_TEMP_BYTES = 8 * MIB
ROW_CHUNK = 16
ROW_CHUNK_UNROLL = 16
AHEAD_CHUNKS = 8

F32 = jnp.float32
BF16 = jnp.bfloat16


def _rmsnorm_rows(x, w):
    return x * lax.rsqrt(jnp.mean(x * x, axis=-1, keepdims=True) + RMS_EPS) * w


def _for_row_chunks(n_rows, fn):
    def body(r, carry):
        fn(pl.ds(pl.multiple_of(r * ROW_CHUNK, ROW_CHUNK), ROW_CHUNK))
        return carry
    lax.fori_loop(0, n_rows // ROW_CHUNK, body, 0, unroll=ROW_CHUNK_UNROLL)


def _sigmoid(x):
    return 1.0 / (1.0 + jnp.exp(-x))


def _nbytes(shape, dtype):
    return math.prod(shape) * jnp.dtype(dtype).itemsize


def _vmem_limit(pipelined, resident):
    estimate = (2 * sum(_nbytes(s, t) for s, t in pipelined)
                + sum(_nbytes(s, t) for s, t in resident) + VMEM_COMPILER_TEMP_BYTES)
    return min(estimate, V7X_VMEM_BYTES - VMEM_RESERVED_BYTES)


def _cast_rows_per_block(rows, n_steps):
    rpb = BF16_SUBLANES
    while rows % rpb or rows // rpb > n_steps:
        rpb += BF16_SUBLANES
        assert rpb <= rows, "no row blocking of this weight fits the grid"
    return rpb


def _cast_specs(weights, grid, first_row=0):
    n_inner = math.prod(grid[1:])
    specs, shapes = [], []
    for w in weights:
        rows, cols = w.shape
        rpb = _cast_rows_per_block(rows, (grid[0] - first_row) * n_inner)
        last = rows // rpb - 1

        def index_map(i, *inner, last=last):
            step = (i - first_row) * n_inner + (inner[0] if inner else 0)
            return (jnp.clip(step, 0, last), 0)

        specs.append(pl.BlockSpec((rpb, cols), index_map))
        shapes.append(jax.ShapeDtypeStruct((rows, cols), BF16))
    return specs, shapes


def _run_casts(src_refs, dst_refs):
    for src, dst in zip(src_refs, dst_refs):
        dst[...] = src[...].astype(dst.dtype)


FFN_DMA_SEMAPHORES = [pltpu.SemaphoreType.DMA(())] * 2

def _ffn_kernel(x_hbm, nw_ref, wg_ref, wu_ref, wd_ref, pw_ref, *rest, post, n_cast, own_cast,
                has_first, norm_ahead):
    rest = list(rest)
    y0_hbm = rest.pop(0) if has_first else None
    n_out = 2 if post == "norm_copy" else 1
    n_w16 = 3 if own_cast else 0
    cast_src, rest = rest[:n_cast], rest[n_cast:]
    outs, rest = rest[:n_out], rest[n_out:]
    w16_refs, rest = rest[:n_w16], rest[n_w16:]
    cast_dst, rest = rest[:n_cast], rest[n_cast:]
    h_ref, r_ref, x_buf, x_sem, o_sem = rest
    o_ref = outs[0]
    i, j = pl.program_id(0), pl.program_id(1)
    tm, d = x_buf.shape
    first = 1 if has_first else 0
    computed = i >= first

    def x_copy(tile):
        return pltpu.make_async_copy(x_hbm.at[pl.ds(tile * tm, tm), :], x_buf, x_sem)

    if has_first:
        y0_copy = pltpu.make_async_copy(y0_hbm, o_ref, o_sem)

        @pl.when((i == 0) & (j == 0))
        def _():
            y0_copy.start()

        @pl.when((i == 0) & (j == 1))
        def _():
            y0_copy.wait()

    @pl.when((i == 0) & (j == 0))
    def _():
        x_copy(first).start()

    n_i, n_j = pl.num_programs(0), pl.num_programs(1)
    has_next = i + 1 < n_i
    if norm_ahead:
        slot = i % 2
        chunk_rows = tm // AHEAD_CHUNKS
        first_chunk_step = n_j - AHEAD_CHUNKS

        @pl.when((i == 0) & (j == 0))
        def _():
            x_copy(0).wait()

            def norm_rows(rows):
                h_ref[0, rows, :] = _rmsnorm_rows(x_buf[rows, :], nw_ref[...]).astype(BF16)
            _for_row_chunks(tm, norm_rows)

        @pl.when(j == 0)
        def _():
            def copy_rows(rows):
                o_ref[rows, :] = x_buf[rows, :]
            _for_row_chunks(tm, copy_rows)

        @pl.when((j == 1) & has_next)
        def _():
            x_copy(i + 1).start()

        @pl.when((j == first_chunk_step) & has_next)
        def _():
            x_copy(i + 1).wait()
    else:
        @pl.when(computed & (j == 0))
        def _():
            x_copy(i).wait()

            def norm_rows(rows):
                x = x_buf[rows, :]
                h_ref[rows, :] = _rmsnorm_rows(x, nw_ref[...]).astype(BF16)
                o_ref[rows, :] = x
            _for_row_chunks(tm, norm_rows)

        @pl.when(computed & (j == n_j // 2) & has_next)
        def _():
            x_copy(i + 1).start()

    def ffn_step(normalise_chunk=False):
        if own_cast:
            wg, wu, wd = (w_ref[...].astype(BF16) for w_ref in (wg_ref, wu_ref, wd_ref))
            for w16_ref, w in zip(w16_refs, (wg, wu, wd)):
                w16_ref[...] = w
        else:
            wg, wu, wd = wg_ref[...], wu_ref[...], wd_ref[...]
        h = h_ref[slot] if norm_ahead else h_ref[...]
        g = jnp.dot(h, wg, preferred_element_type=F32)
        u = jnp.dot(h, wu, preferred_element_type=F32)
        _run_casts(cast_src, cast_dst)
        a = (g * _sigmoid(g)) * (u * FFN_RESIDUAL_SCALE)
        o_ref[...] += jnp.dot(a.astype(BF16), wd, preferred_element_type=F32)
        if normalise_chunk:
            rows = pl.ds(pl.multiple_of((j - first_chunk_step) * chunk_rows, chunk_rows),
                         chunk_rows)
            h_ref[1 - slot, rows, :] = _rmsnorm_rows(x_buf[rows, :], nw_ref[...]).astype(BF16)

    if has_first:
        pl.when(computed)(ffn_step)
    elif norm_ahead:
        pl.when(j < first_chunk_step)(ffn_step)
        pl.when(j >= first_chunk_step)(functools.partial(ffn_step, normalise_chunk=True))
    else:
        ffn_step()

    if post is not None:
        @pl.when(j == pl.num_programs(1) - 1)
        def _():
            if post == "norm_copy":
                def copy_rows(rows):
                    outs[1][rows, :] = _rmsnorm_rows(o_ref[rows, :], pw_ref[...]).astype(BF16)
                _for_row_chunks(tm, copy_rows)
            else:
                def row_stats(rows):
                    y = o_ref[rows, :]
                    ms = jnp.mean(y * y, axis=-1, keepdims=True)
                    r_ref[rows, :] = jnp.broadcast_to(lax.rsqrt(ms + RMS_EPS),
                                                      (ROW_CHUNK, LANES))
                _for_row_chunks(tm, row_stats)

                def scale_rows(rows):
                    r = r_ref[rows, :]
                    o_ref[rows, :] = (o_ref[rows, :] * jnp.concatenate([r] * (d // LANES), axis=1)
                                      * pw_ref[...])
                _for_row_chunks(tm, scale_rows)


def _ffn(x, norm_w, w_gate, w_up, w_down, post_w, *, post, cast_weights=(), first_tile=None,
         norm_ahead=False, tm=1024, tf=512):
    m, d = x.shape
    d_ff = w_gate.shape[1]
    assert m % tm == 0 and d_ff % tf == 0 and post in (None, "norm", "norm_copy")
    grid = (m // tm, d_ff // tf)
    assert grid[1] >= 2, "the x prefetch is started in a later step of each row tile"
    has_first = first_tile is not None
    first = 1 if has_first else 0
    assert not (norm_ahead and has_first)
    assert not norm_ahead or (grid[1] >= AHEAD_CHUNKS + 2 and tm % (AHEAD_CHUNKS * ROW_CHUNK) == 0)
    h_shape = (2, tm, d) if norm_ahead else (tm, d)
    cast_specs, cast_shapes = _cast_specs(cast_weights, grid, first_row=first)
    row_spec = pl.BlockSpec((tm, d), lambda i, j: (i, 0))
    copy_specs = [row_spec] if post == "norm_copy" else []
    copy_shapes = [jax.ShapeDtypeStruct((m, d), BF16)] if post == "norm_copy" else []

    def col(i, j):
        return jnp.where(i < first, 0, j)

    windows = ([((tm, d), F32)] + [((tm, d), BF16)] * len(copy_specs) + [((d, tf), BF16)] * 3
               + [(s.block_shape, F32) for s in cast_specs]
               + [(s.block_shape, BF16) for s in cast_specs])
    scratch = [(h_shape, BF16), ((tm, LANES), F32), ((tm, d), F32)]
    return pl.pallas_call(
        functools.partial(_ffn_kernel, post=post, n_cast=len(cast_specs), own_cast=False,
                          has_first=has_first, norm_ahead=norm_ahead),
        grid=grid,
        in_specs=[
            pl.BlockSpec(memory_space=pl.ANY),
            pl.BlockSpec((1, d), lambda i, j: (0, 0)),
            pl.BlockSpec((d, tf), lambda i, j: (0, col(i, j))),
            pl.BlockSpec((d, tf), lambda i, j: (0, col(i, j))),
            pl.BlockSpec((tf, d), lambda i, j: (col(i, j), 0)),
            pl.BlockSpec((1, d), lambda i, j: (0, 0)),
        ] + [pl.BlockSpec(memory_space=pl.ANY)] * first + cast_specs,
        out_specs=[row_spec] + copy_specs + cast_specs,
        out_shape=[jax.ShapeDtypeStruct((m, d), F32)] + copy_shapes + cast_shapes,
        scratch_shapes=[pltpu.VMEM(s, t) for s, t in scratch] + FFN_DMA_SEMAPHORES,
        compiler_params=pltpu.CompilerParams(
            dimension_semantics=("arbitrary", "arbitrary"),
            vmem_limit_bytes=_vmem_limit(windows, scratch)),
        name="ffn",
    )(x, norm_w, w_gate, w_up, w_down, post_w, *([first_tile] * first), *cast_weights)


def _ffn_head(x, norm_w, w_gate, w_up, w_down, *, tm=1024, tf=256):
    m, d = x.shape
    d_ff = w_gate.shape[1]
    assert m % tm == 0 and d_ff % tf == 0
    grid = (1, d_ff // tf)
    specs = [pl.BlockSpec((d, tf), lambda i, j: (0, j)),
             pl.BlockSpec((d, tf), lambda i, j: (0, j)),
             pl.BlockSpec((tf, d), lambda i, j: (j, 0))]
    windows = [((d, tf), F32)] * 3 + [((d, tf), BF16)] * 3
    tile_spec = pl.BlockSpec((tm, d), lambda i, j: (0, 0), pipeline_mode=pl.Buffered(1))
    scratch = [((tm, d), BF16), ((tm, LANES), F32), ((tm, d), F32)]
    return pl.pallas_call(
        functools.partial(_ffn_kernel, post=None, n_cast=0, own_cast=True, has_first=False,
                          norm_ahead=False),
        grid=grid,
        in_specs=[pl.BlockSpec(memory_space=pl.ANY),
                  pl.BlockSpec((1, d), lambda i, j: (0, 0))] + specs
                 + [pl.BlockSpec((1, d), lambda i, j: (0, 0))],
        out_specs=[tile_spec] + specs,
        out_shape=[jax.ShapeDtypeStruct((tm, d), F32)]
                  + [jax.ShapeDtypeStruct(w.shape, BF16) for w in (w_gate, w_up, w_down)],
        scratch_shapes=[pltpu.VMEM(s, t) for s, t in scratch] + FFN_DMA_SEMAPHORES,
        compiler_params=pltpu.CompilerParams(
            dimension_semantics=("arbitrary", "arbitrary"),
            vmem_limit_bytes=_vmem_limit(windows, scratch + [((tm, d), F32)])),
        name="ffn_head",
    )(x, norm_w, w_gate, w_up, w_down, norm_w)


def _mixer_kernel(hn_ref, hnx_ref, wq_ref, wf_ref, wi_ref, wg_ref, wcb_ref, wcc_ref, wch_ref,
                  lb_ref, nw_ref, cw_ref, *rest, layer, group, n_cast):
    cast_src, (oa_ref, ob_ref, *cast_dst), (p_ref, st_ref, halo_ref) = (
        rest[:n_cast], rest[n_cast:2 * n_cast + 2], rest[2 * n_cast + 2:])
    _mixer_body(hn_ref, hnx_ref, (wq_ref, wf_ref, wi_ref, wg_ref, wcb_ref, wcc_ref, wch_ref),
                lb_ref, nw_ref, cw_ref, cast_src, oa_ref, ob_ref, cast_dst, p_ref, st_ref,
                halo_ref, layer=layer, group=group)


def _mixer_body(hn_ref, hnx_ref, w_refs, lb_ref, nw_ref, cw_ref, cast_src, oa_ref, ob_ref,
                cast_dst, p_ref, st_ref, halo_ref, *, layer, group):
    seq = hn_ref.shape[0]
    slab = group * CHUNK
    n_slabs = seq // slab
    SEC_Q, SEC_F, SEC_I, SEC_G, SEC_CB, SEC_CC, SEC_CH = range(N_SECTIONS)

    logits = lb_ref[...]
    e = jnp.exp(logits - jnp.max(logits, axis=0, keepdims=True))
    lb_all = jnp.sum(e[:layer + 1], axis=0, keepdims=True) / jnp.sum(e, axis=0, keepdims=True)
    nw_all = nw_ref[...]

    causal = (lax.broadcasted_iota(jnp.int32, (CHUNK, CHUNK), 0)
              >= lax.broadcasted_iota(jnp.int32, (CHUNK, CHUNK), 1))
    srow_i = lax.broadcasted_iota(jnp.int32, (CUMSUM_ROWS, CUMSUM_ROWS), 0)
    scol_i = lax.broadcasted_iota(jnp.int32, (CUMSUM_ROWS, CUMSUM_ROWS), 1)
    tril16 = ((srow_i >= scol_i) & (srow_i // CHUNK == scol_i // CHUNK)).astype(BF16)
    top_row = lax.broadcasted_iota(jnp.int32, (SUBLANES, LANES), 0)
    heads_in_step = range(PAIR)
    lanes = [slice(hh * HEAD_DIM, (hh + 1) * HEAD_DIM) for hh in heads_in_step]
    chunks = [slice(c * CHUNK, (c + 1) * CHUNK) for c in range(group)]

    def project_tasks(load_rows, slot):
        def task(s):
            p_ref[slot, s] = jnp.dot(load_rows(), w_refs[s][...], preferred_element_type=F32)
        return [functools.partial(task, s) for s in range(N_SECTIONS)]

    def mix_tasks(row0, slot):
        v = {}
        rows = pl.ds(row0, slab)

        def sec(s, hh):
            return p_ref[slot, s, :, lanes[hh]]

        def gates():
            v["hl"], v["k"] = [], []
            for hh in heads_in_step:
                lb = lb_all[:, lanes[hh]]
                f = lb + (1.0 - lb) * _sigmoid(sec(SEC_F, hh))
                log_f = jnp.log(f)
                v["k"].append(1.0 - f)
                hi = log_f.astype(BF16)
                lo = (log_f - hi.astype(F32)).astype(BF16)
                v["hl"].append(jnp.concatenate([hi, lo], axis=1))

        def cumsum():
            v["b"] = []
            for hh in heads_in_step:
                bb = jnp.concatenate(
                    [jnp.dot(tril16, v["hl"][hh][r0:r0 + CUMSUM_ROWS],
                             preferred_element_type=F32)
                     for r0 in range(0, slab, CUMSUM_ROWS)], axis=0)
                v["b"].append(bb[:, :HEAD_DIM] + bb[:, HEAD_DIM:])

        def decays():
            v["q_dec"], v["k_dec"], v["k_state"], v["decay"], v["v16"] = [], [], [], [], []
            for hh in heads_in_step:
                b = v["b"][hh]
                eb = jnp.exp(b)
                v["q_dec"].append((sec(SEC_Q, hh) * eb).astype(BF16))
                k_dec32 = v["k"][hh] * (1.0 / eb)
                v["k_dec"].append(k_dec32.astype(BF16))
                dec = [jnp.exp(b[ch.stop - 1:ch.stop, :]) for ch in chunks]
                v["decay"].append(dec)
                v["k_state"].append([(k_dec32[ch] * d).astype(BF16)
                                     for ch, d in zip(chunks, dec)])
                v["v16"].append(sec(SEC_I, hh).astype(BF16))

        def chunk_dots():
            v["scores"], v["update"] = [], []
            for hh in heads_in_step:
                v32 = sec(SEC_I, hh)
                v["scores"].append([lax.dot_general(v["q_dec"][hh][ch], v["k_dec"][hh][ch],
                                                    (((1,), (1,)), ((), ())),
                                                    preferred_element_type=F32)
                                    for ch in chunks])
                v["update"].append([jnp.dot(v32[ch].T.astype(BF16), v["k_state"][hh][c],
                                            preferred_element_type=F32)
                                    for c, ch in enumerate(chunks)])

        def read_out():
            v["o"] = []
            for hh in heads_in_step:
                o_intra = [jnp.dot(jnp.where(causal, v["scores"][hh][c], 0.0).astype(BF16),
                                   v["v16"][hh][ch], preferred_element_type=F32)
                           for c, ch in enumerate(chunks)]
                st_t = st_ref[hh]
                o_inter = []
                for c, ch in enumerate(chunks):
                    o_inter.append(lax.dot_general(v["q_dec"][hh][ch], st_t.astype(BF16),
                                                   (((1,), (1,)), ((), ())),
                                                   preferred_element_type=F32))
                    st_t = st_t * v["decay"][hh][c] + v["update"][hh][c]
                st_ref[hh] = st_t
                v["o"].append(jnp.concatenate([a + b for a, b in zip(o_intra, o_inter)], axis=0))

        def norm_gate():
            for hh in heads_in_step:
                o = v["o"][hh]
                gate = sec(SEC_G, hh)
                o = (o * lax.rsqrt(jnp.mean(o * o, axis=-1, keepdims=True) + RMS_EPS)
                     * nw_all[:, lanes[hh]])
                o = o * (gate * _sigmoid(gate))
                oa_ref[rows, lanes[hh]] = o.astype(oa_ref.dtype)

        def conv():
            for hh in heads_in_step:
                cw = cw_ref[:, lanes[hh]]
                u_in = sec(SEC_CC, hh) * sec(SEC_CH, hh)
                halo = halo_ref[hh]
                y = cw[CONV_K - 1:CONV_K, :] * u_in
                for tap in range(CONV_K - 1):
                    shift = CONV_K - 1 - tap
                    shifted = pltpu.roll(u_in, shift, axis=0)
                    top = shifted[:SUBLANES]
                    for r0 in range(shift):
                        hrow = SUBLANES - shift + r0
                        top = jnp.where(top_row == r0, halo[hrow:hrow + 1, :], top)
                    shifted = jnp.concatenate([top, shifted[SUBLANES:]], axis=0)
                    y = y + cw[tap:tap + 1, :] * shifted
                halo_ref[hh] = u_in[slab - SUBLANES:, :]
                ob_ref[rows, lanes[hh]] = (sec(SEC_CB, hh) * y).astype(ob_ref.dtype)

        return [gates, cumsum, decays, chunk_dots, read_out, norm_gate, conv]

    def run_interleaved(mix_list, proj_list):
        n = max(len(mix_list), len(proj_list))
        for t in range(n):
            for tasks in (proj_list, mix_list):
                for task in tasks[t * len(tasks) // n:(t + 1) * len(tasks) // n]:
                    task()

    @pl.when(pl.program_id(1) == 0)
    def _():
        for task in project_tasks(lambda: hn_ref[0:slab, :], 0):
            task()

    st_ref[...] = jnp.zeros_like(st_ref)
    halo_ref[...] = jnp.zeros_like(halo_ref)

    def slab_pair(i, carry):
        row0 = pl.multiple_of(i * (2 * slab), 2 * slab)
        run_interleaved(mix_tasks(row0, 0),
                        project_tasks(lambda: hn_ref[pl.ds(row0 + slab, slab), :], 1))
        run_interleaved(mix_tasks(row0 + slab, 1),
                        project_tasks(lambda: hn_ref[pl.ds(row0 + 2 * slab, slab), :], 0))
        return carry

    lax.fori_loop(0, n_slabs // 2 - 1, slab_pair, 0)
    row0 = (n_slabs - 2) * slab
    run_interleaved(mix_tasks(row0, 0),
                    project_tasks(lambda: hn_ref[row0 + slab:row0 + 2 * slab, :], 1))
    run_interleaved(mix_tasks(row0 + slab, 1), project_tasks(lambda: hnx_ref[...], 0))
    _run_casts(cast_src, cast_dst)


def _mixer(hn, w_in, lb_logits, out_norm_w, conv_w, *, layer, batch, heads, cast_weights=(),
           group=8):
    m, d = hn.shape
    seq = m // batch
    width = heads * HEAD_DIM
    pw = PAIR * HEAD_DIM
    n_pairs = heads // PAIR
    slab = group * CHUNK
    n_slabs = seq // slab
    assert heads % PAIR == 0 and w_in.shape == (d, N_SECTIONS * width)
    assert seq % slab == 0 and n_slabs % 2 == 0 and n_slabs >= 4 and CONV_K - 1 <= SUBLANES
    assert slab % CUMSUM_ROWS == 0 and CUMSUM_ROWS % CHUNK == 0

    def w_spec(section):
        return pl.BlockSpec((d, pw), lambda p, b, s=section: (0, s * n_pairs + p),
                            pipeline_mode=pl.Buffered(1))

    def vec_spec(rows):
        return pl.BlockSpec((rows, pw), lambda p, b: (0, p))

    grid = (n_pairs, batch)
    cast_specs, cast_shapes = _cast_specs(cast_weights, grid)
    out_spec = pl.BlockSpec((seq, pw), lambda p, b: (b, p))
    windows = ([((seq, d), BF16), ((slab, d), BF16)] + [((seq, pw), BF16)] * 2
               + [(s.block_shape, F32) for s in cast_specs]
               + [(s.block_shape, BF16) for s in cast_specs])
    slab_proj = ((N_SECTIONS, slab, pw), F32)
    resident = ([((d, pw), BF16)] * N_SECTIONS + [slab_proj] * 3
                + [((PAIR, HEAD_DIM, HEAD_DIM), F32), ((PAIR, SUBLANES, LANES), F32)])
    return pl.pallas_call(
        functools.partial(_mixer_kernel, layer=layer, group=group, n_cast=len(cast_specs)),
        grid=grid,
        in_specs=[pl.BlockSpec((seq, d), lambda p, b: (b, 0)),
                  pl.BlockSpec((slab, d), lambda p, b: (((b + 1) % batch) * n_slabs, 0))]
                 + [w_spec(s) for s in range(N_SECTIONS)]
                 + [vec_spec(lb_logits.shape[0]), vec_spec(1), vec_spec(CONV_K)]
                 + cast_specs,
        out_specs=[out_spec, out_spec] + cast_specs,
        out_shape=[jax.ShapeDtypeStruct((m, width), BF16)] * 2 + cast_shapes,
        scratch_shapes=[pltpu.VMEM((2,) + slab_proj[0], F32),
                        pltpu.VMEM((PAIR, HEAD_DIM, HEAD_DIM), F32),
                        pltpu.VMEM((PAIR, SUBLANES, LANES), F32)],
        compiler_params=pltpu.CompilerParams(
            dimension_semantics=("arbitrary", "arbitrary"),
            vmem_limit_bytes=_vmem_limit(windows, resident)),
        name="mixer",
    )(hn, hn, *([w_in] * N_SECTIONS), lb_logits, out_norm_w, conv_w, *cast_weights)


def _outproj_kernel(oa_ref, ob_ref, x_ref, wa_ref, wb_ref, o_ref):
    o_ref[...] = (x_ref[...]
                  + jnp.dot(oa_ref[...], wa_ref[...], preferred_element_type=F32)
                  + jnp.dot(ob_ref[...], wb_ref[...], preferred_element_type=F32))


def _out_proj(oa, ob, x, w_out, *, tm=512):
    m, width = oa.shape
    d_mix, d = w_out.shape
    assert d_mix == 2 * width and m % tm == 0
    return pl.pallas_call(
        _outproj_kernel,
        grid=(m // tm,),
        in_specs=[
            pl.BlockSpec((tm, width), lambda i: (i, 0)),
            pl.BlockSpec((tm, width), lambda i: (i, 0)),
            pl.BlockSpec((tm, d), lambda i: (i, 0)),
            pl.BlockSpec((width, d), lambda i: (0, 0)),
            pl.BlockSpec((width, d), lambda i: (1, 0)),
        ],
        out_specs=pl.BlockSpec((tm, d), lambda i: (i, 0)),
        out_shape=jax.ShapeDtypeStruct((m, d), F32),
        compiler_params=pltpu.CompilerParams(
            dimension_semantics=("parallel",),
            vmem_limit_bytes=_vmem_limit(
                [((tm, width), BF16)] * 2 + [((tm, d), F32)] * 2 + [((width, d), BF16)] * 2, [])),
        name="out_proj",
    )(oa, ob, x, w_out, w_out)


def kernel(x, ffn1_norm_w, ffn1_w_gate, ffn1_w_up, ffn1_w_down, mix_norm_w, w_in_mix,
           hgrn_lb_logits, hgrn_out_norm_w, conv_w, w_out_mix, ffn2_norm_w, ffn2_w_gate,
           ffn2_w_up, ffn2_w_down, final_norm_w):
    batch, seq, d = x.shape
    depth = ffn1_norm_w.shape[0]
    m = batch * seq
    heads = hgrn_out_norm_w.shape[1] // HEAD_DIM
    assert depth >= 1 and conv_w.shape[2] == heads * LANES
    final_w = final_norm_w.reshape(1, d)

    y = x.reshape(m, d)
    for l in range(depth):
        last = l == depth - 1
        y0, w1_gate, w1_up, w1_down = _ffn_head(
            y, ffn1_norm_w[l].reshape(1, d), ffn1_w_gate[l], ffn1_w_up[l], ffn1_w_down[l])
        y, hn, w_in, w2_gate, w2_up = _ffn(
            y, ffn1_norm_w[l].reshape(1, d), w1_gate, w1_up, w1_down,
            mix_norm_w[l].reshape(1, d), post="norm_copy", first_tile=y0,
            cast_weights=(w_in_mix[l], ffn2_w_gate[l], ffn2_w_up[l]))
        oa, ob, w2_down, w_out = _mixer(
            hn, w_in, hgrn_lb_logits, hgrn_out_norm_w[l].reshape(1, -1), conv_w[l],
            layer=l, batch=batch, heads=heads, cast_weights=(ffn2_w_down[l], w_out_mix[l]))
        y = _out_proj(oa, ob, y, w_out)
        (y,) = _ffn(y, ffn2_norm_w[l].reshape(1, d), w2_gate, w2_up, w2_down, final_w,
                    post="norm" if last else None, norm_ahead=True)
    return y.reshape(batch, seq, d)
```

```python
import functools
import math

import jax
import jax.numpy as jnp
from jax import lax
from jax.experimental import pallas as pl
from jax.experimental.pallas import tpu as pltpu

LANES = 128
SUBLANES = 8
CHUNK = 64
HEAD_DIM = 128
PAIR = 2
N_SECTIONS = 7
CONV_K = 3
CUMSUM_ROWS = 256
RMS_EPS = 1e-6
FFN_RESIDUAL_SCALE = 0.5
MIB = 1024 * 1024
BF16_SUBLANES = 16
V7X_VMEM_BYTES = 64 * MIB
VMEM_RESERVED_BYTES = 4 * MIB
VMEM_COMPILER_TEMP_BYTES = 8 * MIB
ROW_CHUNK = 16
ROW_CHUNK_UNROLL = 16
AHEAD_CHUNKS = 8

F32 = jnp.float32
BF16 = jnp.bfloat16


def _rmsnorm_rows(x, w):
    return x * lax.rsqrt(jnp.mean(x * x, axis=-1, keepdims=True) + RMS_EPS) * w


def _for_row_chunks(n_rows, fn):
    def body(r, carry):
        fn(pl.ds(pl.multiple_of(r * ROW_CHUNK, ROW_CHUNK), ROW_CHUNK))
        return carry
    lax.fori_loop(0, n_rows // ROW_CHUNK, body, 0, unroll=ROW_CHUNK_UNROLL)


def _sigmoid(x):
    return 1.0 / (1.0 + jnp.exp(-x))


def _nbytes(shape, dtype):
    return math.prod(shape) * jnp.dtype(dtype).itemsize


def _vmem_limit(pipelined, resident):
    estimate = (2 * sum(_nbytes(s, t) for s, t in pipelined)
                + sum(_nbytes(s, t) for s, t in resident) + VMEM_COMPILER_TEMP_BYTES)
    return min(estimate, V7X_VMEM_BYTES - VMEM_RESERVED_BYTES)


def _cast_rows_per_block(rows, n_steps):
    rpb = BF16_SUBLANES
    while rows % rpb or rows // rpb > n_steps:
        rpb += BF16_SUBLANES
        assert rpb <= rows, "no row blocking of this weight fits the grid"
    return rpb


def _cast_specs(weights, grid, first_row=0):
    n_inner = math.prod(grid[1:])
    specs, shapes = [], []
    for w in weights:
        rows, cols = w.shape
        rpb = _cast_rows_per_block(rows, (grid[0] - first_row) * n_inner)
        last = rows // rpb - 1

        def index_map(i, *inner, last=last):
            step = (i - first_row) * n_inner + (inner[0] if inner else 0)
            return (jnp.clip(step, 0, last), 0)

        specs.append(pl.BlockSpec((rpb, cols), index_map))
        shapes.append(jax.ShapeDtypeStruct((rows, cols), BF16))
    return specs, shapes


def _run_casts(src_refs, dst_refs):
    for src, dst in zip(src_refs, dst_refs):
        dst[...] = src[...].astype(dst.dtype)


FFN_DMA_SEMAPHORES = [pltpu.SemaphoreType.DMA(())] * 2

def _ffn_kernel(x_hbm, nw_ref, wg_ref, wu_ref, wd_ref, pw_ref, *rest, post, n_cast, own_cast,
                has_first, norm_ahead):
    rest = list(rest)
    y0_hbm = rest.pop(0) if has_first else None
    n_out = 2 if post == "norm_copy" else 1
    n_w16 = 3 if own_cast else 0
    cast_src, rest = rest[:n_cast], rest[n_cast:]
    outs, rest = rest[:n_out], rest[n_out:]
    w16_refs, rest = rest[:n_w16], rest[n_w16:]
    cast_dst, rest = rest[:n_cast], rest[n_cast:]
    h_ref, r_ref, x_buf, x_sem, o_sem = rest
    o_ref = outs[0]
    i, j = pl.program_id(0), pl.program_id(1)
    tm, d = x_buf.shape
    first = 1 if has_first else 0
    computed = i >= first

    def x_copy(tile):
        return pltpu.make_async_copy(x_hbm.at[pl.ds(tile * tm, tm), :], x_buf, x_sem)

    if has_first:
        y0_copy = pltpu.make_async_copy(y0_hbm, o_ref, o_sem)

        @pl.when((i == 0) & (j == 0))
        def _():
            y0_copy.start()

        @pl.when((i == 0) & (j == 1))
        def _():
            y0_copy.wait()

    @pl.when((i == 0) & (j == 0))
    def _():
        x_copy(first).start()

    n_i, n_j = pl.num_programs(0), pl.num_programs(1)
    has_next = i + 1 < n_i
    if norm_ahead:
        slot = i % 2
        chunk_rows = tm // AHEAD_CHUNKS
        first_chunk_step = n_j - AHEAD_CHUNKS

        @pl.when((i == 0) & (j == 0))
        def _():
            x_copy(0).wait()

            def norm_rows(rows):
                h_ref[0, rows, :] = _rmsnorm_rows(x_buf[rows, :], nw_ref[...]).astype(BF16)
            _for_row_chunks(tm, norm_rows)

        @pl.when((j == 1) & has_next)
        def _():
            x_copy(i + 1).start()

        @pl.when((j == first_chunk_step) & has_next)
        def _():
            x_copy(i + 1).wait()
    else:
        @pl.when(computed & (j == 0))
        def _():
            x_copy(i).wait()

            def norm_rows(rows):
                x = x_buf[rows, :]
                h_ref[rows, :] = _rmsnorm_rows(x, nw_ref[...]).astype(BF16)
                o_ref[rows, :] = x
            _for_row_chunks(tm, norm_rows)

        @pl.when(computed & (j == n_j // 2) & has_next)
        def _():
            x_copy(i + 1).start()

    def ffn_step(normalise_chunk=False, first_step=False):
        if own_cast:
            wg, wu, wd = (w_ref[...].astype(BF16) for w_ref in (wg_ref, wu_ref, wd_ref))
            for w16_ref, w in zip(w16_refs, (wg, wu, wd)):
                w16_ref[...] = w
        else:
            wg, wu, wd = wg_ref[...], wu_ref[...], wd_ref[...]
        h = h_ref[slot] if norm_ahead else h_ref[...]
        g = jnp.dot(h, wg, preferred_element_type=F32)
        u = jnp.dot(h, wu, preferred_element_type=F32)
        _run_casts(cast_src, cast_dst)
        a = (g * _sigmoid(g)) * (u * FFN_RESIDUAL_SCALE)
        down = jnp.dot(a.astype(BF16), wd, preferred_element_type=F32)
        if first_step:
            o_ref[...] = x_buf[...] + down
        else:
            o_ref[...] += down
        if normalise_chunk:
            rows = pl.ds(pl.multiple_of((j - first_chunk_step) * chunk_rows, chunk_rows),
                         chunk_rows)
            h_ref[1 - slot, rows, :] = _rmsnorm_rows(x_buf[rows, :], nw_ref[...]).astype(BF16)

    if has_first:
        pl.when(computed)(ffn_step)
    elif norm_ahead:
        pl.when(j == 0)(functools.partial(ffn_step, first_step=True))
        pl.when((j > 0) & (j < first_chunk_step))(ffn_step)
        pl.when(j >= first_chunk_step)(functools.partial(ffn_step, normalise_chunk=True))
    else:
        ffn_step()

    if post is not None:
        @pl.when(j == pl.num_programs(1) - 1)
        def _():
            if post == "norm_copy":
                def copy_rows(rows):
                    outs[1][rows, :] = _rmsnorm_rows(o_ref[rows, :], pw_ref[...]).astype(BF16)
                _for_row_chunks(tm, copy_rows)
            else:
                def row_stats(rows):
                    y = o_ref[rows, :]
                    ms = jnp.mean(y * y, axis=-1, keepdims=True)
                    r_ref[rows, :] = jnp.broadcast_to(lax.rsqrt(ms + RMS_EPS),
                                                      (ROW_CHUNK, LANES))
                _for_row_chunks(tm, row_stats)

                def scale_rows(rows):
                    r = r_ref[rows, :]
                    o_ref[rows, :] = (o_ref[rows, :] * jnp.concatenate([r] * (d // LANES), axis=1)
                                      * pw_ref[...])
                _for_row_chunks(tm, scale_rows)


def _ffn(x, norm_w, w_gate, w_up, w_down, post_w, *, post, cast_weights=(), first_tile=None,
         norm_ahead=False, tm=1024, tf=512):
    m, d = x.shape
    d_ff = w_gate.shape[1]
    assert m % tm == 0 and d_ff % tf == 0 and post in (None, "norm", "norm_copy")
    grid = (m // tm, d_ff // tf)
    assert grid[1] >= 2, "the x prefetch is started in a later step of each row tile"
    has_first = first_tile is not None
    first = 1 if has_first else 0
    assert not (norm_ahead and has_first)
    assert not norm_ahead or (grid[1] >= AHEAD_CHUNKS + 2 and tm % (AHEAD_CHUNKS * ROW_CHUNK) == 0)
    h_shape = (2, tm, d) if norm_ahead else (tm, d)
    cast_specs, cast_shapes = _cast_specs(cast_weights, grid, first_row=first)
    row_spec = pl.BlockSpec((tm, d), lambda i, j: (i, 0))
    copy_specs = [row_spec] if post == "norm_copy" else []
    copy_shapes = [jax.ShapeDtypeStruct((m, d), BF16)] if post == "norm_copy" else []

    def col(i, j):
        return jnp.where(i < first, 0, j)

    windows = ([((tm, d), F32)] + [((tm, d), BF16)] * len(copy_specs) + [((d, tf), BF16)] * 3
               + [(s.block_shape, F32) for s in cast_specs]
               + [(s.block_shape, BF16) for s in cast_specs])
    scratch = [(h_shape, BF16), ((tm, LANES), F32), ((tm, d), F32)]
    return pl.pallas_call(
        functools.partial(_ffn_kernel, post=post, n_cast=len(cast_specs), own_cast=False,
                          has_first=has_first, norm_ahead=norm_ahead),
        grid=grid,
        in_specs=[
            pl.BlockSpec(memory_space=pl.ANY),
            pl.BlockSpec((1, d), lambda i, j: (0, 0)),
            pl.BlockSpec((d, tf), lambda i, j: (0, col(i, j))),
            pl.BlockSpec((d, tf), lambda i, j: (0, col(i, j))),
            pl.BlockSpec((tf, d), lambda i, j: (col(i, j), 0)),
            pl.BlockSpec((1, d), lambda i, j: (0, 0)),
        ] + [pl.BlockSpec(memory_space=pl.ANY)] * first + cast_specs,
        out_specs=[row_spec] + copy_specs + cast_specs,
        out_shape=[jax.ShapeDtypeStruct((m, d), F32)] + copy_shapes + cast_shapes,
        scratch_shapes=[pltpu.VMEM(s, t) for s, t in scratch] + FFN_DMA_SEMAPHORES,
        compiler_params=pltpu.CompilerParams(
            dimension_semantics=("arbitrary", "arbitrary"),
            vmem_limit_bytes=_vmem_limit(windows, scratch)),
        name="ffn",
    )(x, norm_w, w_gate, w_up, w_down, post_w, *([first_tile] * first), *cast_weights)


def _ffn_head(x, norm_w, w_gate, w_up, w_down, *, tm=1024, tf=256):
    m, d = x.shape
    d_ff = w_gate.shape[1]
    assert m % tm == 0 and d_ff % tf == 0
    grid = (1, d_ff // tf)
    specs = [pl.BlockSpec((d, tf), lambda i, j: (0, j)),
             pl.BlockSpec((d, tf), lambda i, j: (0, j)),
             pl.BlockSpec((tf, d), lambda i, j: (j, 0))]
    windows = [((d, tf), F32)] * 3 + [((d, tf), BF16)] * 3
    tile_spec = pl.BlockSpec((tm, d), lambda i, j: (0, 0), pipeline_mode=pl.Buffered(1))
    scratch = [((tm, d), BF16), ((tm, LANES), F32), ((tm, d), F32)]
    return pl.pallas_call(
        functools.partial(_ffn_kernel, post=None, n_cast=0, own_cast=True, has_first=False,
                          norm_ahead=False),
        grid=grid,
        in_specs=[pl.BlockSpec(memory_space=pl.ANY),
                  pl.BlockSpec((1, d), lambda i, j: (0, 0))] + specs
                 + [pl.BlockSpec((1, d), lambda i, j: (0, 0))],
        out_specs=[tile_spec] + specs,
        out_shape=[jax.ShapeDtypeStruct((tm, d), F32)]
                  + [jax.ShapeDtypeStruct(w.shape, BF16) for w in (w_gate, w_up, w_down)],
        scratch_shapes=[pltpu.VMEM(s, t) for s, t in scratch] + FFN_DMA_SEMAPHORES,
        compiler_params=pltpu.CompilerParams(
            dimension_semantics=("arbitrary", "arbitrary"),
            vmem_limit_bytes=_vmem_limit(windows, scratch + [((tm, d), F32)])),
        name="ffn_head",
    )(x, norm_w, w_gate, w_up, w_down, norm_w)


def _mixer_kernel(hn_ref, hnx_ref, wq_ref, wf_ref, wi_ref, wg_ref, wcb_ref, wcc_ref, wch_ref,
                  lb_ref, nw_ref, cw_ref, *rest, layer, group, n_cast):
    cast_src, (oa_ref, ob_ref, *cast_dst), (p_ref, st_ref, halo_ref) = (
        rest[:n_cast], rest[n_cast:2 * n_cast + 2], rest[2 * n_cast + 2:])
    _mixer_body(hn_ref, hnx_ref, (wq_ref, wf_ref, wi_ref, wg_ref, wcb_ref, wcc_ref, wch_ref),
                lb_ref, nw_ref, cw_ref, cast_src, oa_ref, ob_ref, cast_dst, p_ref, st_ref,
                halo_ref, layer=layer, group=group)


def _mixer_body(hn_ref, hnx_ref, w_refs, lb_ref, nw_ref, cw_ref, cast_src, oa_ref, ob_ref,
                cast_dst, p_ref, st_ref, halo_ref, *, layer, group):
    seq = hn_ref.shape[0]
    slab = group * CHUNK
    n_slabs = seq // slab
    SEC_Q, SEC_F, SEC_I, SEC_G, SEC_CB, SEC_CC, SEC_CH = range(N_SECTIONS)

    logits = lb_ref[...]
    e = jnp.exp(logits - jnp.max(logits, axis=0, keepdims=True))
    lb_all = jnp.sum(e[:layer + 1], axis=0, keepdims=True) / jnp.sum(e, axis=0, keepdims=True)
    nw_all = nw_ref[...]

    causal = (lax.broadcasted_iota(jnp.int32, (CHUNK, CHUNK), 0)
              >= lax.broadcasted_iota(jnp.int32, (CHUNK, CHUNK), 1))
    srow_i = lax.broadcasted_iota(jnp.int32, (CUMSUM_ROWS, CUMSUM_ROWS), 0)
    scol_i = lax.broadcasted_iota(jnp.int32, (CUMSUM_ROWS, CUMSUM_ROWS), 1)
    tril16 = ((srow_i >= scol_i) & (srow_i // CHUNK == scol_i // CHUNK)).astype(BF16)
    top_row = lax.broadcasted_iota(jnp.int32, (SUBLANES, LANES), 0)
    heads_in_step = range(PAIR)
    lanes = [slice(hh * HEAD_DIM, (hh + 1) * HEAD_DIM) for hh in heads_in_step]
    chunks = [slice(c * CHUNK, (c + 1) * CHUNK) for c in range(group)]

    def project_tasks(load_rows, slot):
        def task(s):
            p_ref[slot, s] = jnp.dot(load_rows(), w_refs[s][...], preferred_element_type=F32)
        return [functools.partial(task, s) for s in range(N_SECTIONS)]

    def mix_tasks(row0, slot):
        v = {}
        rows = pl.ds(row0, slab)

        def sec(s, hh):
            return p_ref[slot, s, :, lanes[hh]]

        def gates():
            v["hl"], v["k"] = [], []
            for hh in heads_in_step:
                lb = lb_all[:, lanes[hh]]
                f = lb + (1.0 - lb) * _sigmoid(sec(SEC_F, hh))
                log_f = jnp.log(f)
                v["k"].append(1.0 - f)
                hi = log_f.astype(BF16)
                lo = (log_f - hi.astype(F32)).astype(BF16)
                v["hl"].append(jnp.concatenate([hi, lo], axis=1))

        def cumsum():
            v["b"] = []
            for hh in heads_in_step:
                bb = jnp.concatenate(
                    [jnp.dot(tril16, v["hl"][hh][r0:r0 + CUMSUM_ROWS],
                             preferred_element_type=F32)
                     for r0 in range(0, slab, CUMSUM_ROWS)], axis=0)
                v["b"].append(bb[:, :HEAD_DIM] + bb[:, HEAD_DIM:])

        def decays():
            v["q_dec"], v["k_dec"], v["k_state"], v["decay"], v["v16"] = [], [], [], [], []
            for hh in heads_in_step:
                b = v["b"][hh]
                eb = jnp.exp(b)
                v["q_dec"].append((sec(SEC_Q, hh) * eb).astype(BF16))
                k_dec32 = v["k"][hh] * (1.0 / eb)
                v["k_dec"].append(k_dec32.astype(BF16))
                dec = [jnp.exp(b[ch.stop - 1:ch.stop, :]) for ch in chunks]
                v["decay"].append(dec)
                v["k_state"].append([(k_dec32[ch] * d).astype(BF16)
                                     for ch, d in zip(chunks, dec)])
                v["v16"].append(sec(SEC_I, hh).astype(BF16))

        def chunk_dots():
            v["scores"], v["update"] = [], []
            for hh in heads_in_step:
                v32 = sec(SEC_I, hh)
                v["scores"].append([lax.dot_general(v["q_dec"][hh][ch], v["k_dec"][hh][ch],
                                                    (((1,), (1,)), ((), ())),
                                                    preferred_element_type=F32)
                                    for ch in chunks])
                v["update"].append([jnp.dot(v32[ch].T.astype(BF16), v["k_state"][hh][c],
                                            preferred_element_type=F32)
                                    for c, ch in enumerate(chunks)])

        def read_out():
            v["o"] = []
            for hh in heads_in_step:
                o_intra = [jnp.dot(jnp.where(causal, v["scores"][hh][c], 0.0).astype(BF16),
                                   v["v16"][hh][ch], preferred_element_type=F32)
                           for c, ch in enumerate(chunks)]
                st_t = st_ref[hh]
                o_inter = []
                for c, ch in enumerate(chunks):
                    o_inter.append(lax.dot_general(v["q_dec"][hh][ch], st_t.astype(BF16),
                                                   (((1,), (1,)), ((), ())),
                                                   preferred_element_type=F32))
                    st_t = st_t * v["decay"][hh][c] + v["update"][hh][c]
                st_ref[hh] = st_t
                v["o"].append(jnp.concatenate([a + b for a, b in zip(o_intra, o_inter)], axis=0))

        def norm_gate():
            for hh in heads_in_step:
                o = v["o"][hh]
                gate = sec(SEC_G, hh)
                o = (o * lax.rsqrt(jnp.mean(o * o, axis=-1, keepdims=True) + RMS_EPS)
                     * nw_all[:, lanes[hh]])
                o = o * (gate * _sigmoid(gate))
                oa_ref[rows, lanes[hh]] = o.astype(oa_ref.dtype)

        def conv():
            for hh in heads_in_step:
                cw = cw_ref[:, lanes[hh]]
                u_in = sec(SEC_CC, hh) * sec(SEC_CH, hh)
                halo = halo_ref[hh]
                y = cw[CONV_K - 1:CONV_K, :] * u_in
                for tap in range(CONV_K - 1):
                    shift = CONV_K - 1 - tap
                    shifted = pltpu.roll(u_in, shift, axis=0)
                    top = shifted[:SUBLANES]
                    for r0 in range(shift):
                        hrow = SUBLANES - shift + r0
                        top = jnp.where(top_row == r0, halo[hrow:hrow + 1, :], top)
                    shifted = jnp.concatenate([top, shifted[SUBLANES:]], axis=0)
                    y = y + cw[tap:tap + 1, :] * shifted
                halo_ref[hh] = u_in[slab - SUBLANES:, :]
                ob_ref[rows, lanes[hh]] = (sec(SEC_CB, hh) * y).astype(ob_ref.dtype)

        return [gates, cumsum, decays, chunk_dots, read_out, norm_gate, conv]

    def run_interleaved(mix_list, proj_list):
        n = max(len(mix_list), len(proj_list))
        for t in range(n):
            for tasks in (proj_list, mix_list):
                for task in tasks[t * len(tasks) // n:(t + 1) * len(tasks) // n]:
                    task()

    @pl.when(pl.program_id(1) == 0)
    def _():
        for task in project_tasks(lambda: hn_ref[0:slab, :], 0):
            task()

    st_ref[...] = jnp.zeros_like(st_ref)
    halo_ref[...] = jnp.zeros_like(halo_ref)

    def slab_pair(i, carry):
        row0 = pl.multiple_of(i * (2 * slab), 2 * slab)
        run_interleaved(mix_tasks(row0, 0),
                        project_tasks(lambda: hn_ref[pl.ds(row0 + slab, slab), :], 1))
        run_interleaved(mix_tasks(row0 + slab, 1),
                        project_tasks(lambda: hn_ref[pl.ds(row0 + 2 * slab, slab), :], 0))
        return carry

    lax.fori_loop(0, n_slabs // 2 - 1, slab_pair, 0)
    row0 = (n_slabs - 2) * slab
    run_interleaved(mix_tasks(row0, 0),
                    project_tasks(lambda: hn_ref[row0 + slab:row0 + 2 * slab, :], 1))
    run_interleaved(mix_tasks(row0 + slab, 1), project_tasks(lambda: hnx_ref[...], 0))
    _run_casts(cast_src, cast_dst)


def _mixer(hn, w_in, lb_logits, out_norm_w, conv_w, *, layer, batch, heads, cast_weights=(),
           group=8):
    m, d = hn.shape
    seq = m // batch
    width = heads * HEAD_DIM
    pw = PAIR * HEAD_DIM
    n_pairs = heads // PAIR
    slab = group * CHUNK
    n_slabs = seq // slab
    assert heads % PAIR == 0 and w_in.shape == (d, N_SECTIONS * width)
    assert seq % slab == 0 and n_slabs % 2 == 0 and n_slabs >= 4 and CONV_K - 1 <= SUBLANES
    assert slab % CUMSUM_ROWS == 0 and CUMSUM_ROWS % CHUNK == 0

    def w_spec(section):
        return pl.BlockSpec((d, pw), lambda p, b, s=section: (0, s * n_pairs + p),
                            pipeline_mode=pl.Buffered(1))

    def vec_spec(rows):
        return pl.BlockSpec((rows, pw), lambda p, b: (0, p))

    grid = (n_pairs, batch)
    cast_specs, cast_shapes = _cast_specs(cast_weights, grid)
    out_spec = pl.BlockSpec((seq, pw), lambda p, b: (b, p))
    windows = ([((seq, d), BF16), ((slab, d), BF16)] + [((seq, pw), BF16)] * 2
               + [(s.block_shape, F32) for s in cast_specs]
               + [(s.block_shape, BF16) for s in cast_specs])
    slab_proj = ((N_SECTIONS, slab, pw), F32)
    resident = ([((d, pw), BF16)] * N_SECTIONS + [slab_proj] * 3
                + [((PAIR, HEAD_DIM, HEAD_DIM), F32), ((PAIR, SUBLANES, LANES), F32)])
    return pl.pallas_call(
        functools.partial(_mixer_kernel, layer=layer, group=group, n_cast=len(cast_specs)),
        grid=grid,
        in_specs=[pl.BlockSpec((seq, d), lambda p, b: (b, 0)),
                  pl.BlockSpec((slab, d), lambda p, b: (((b + 1) % batch) * n_slabs, 0))]
                 + [w_spec(s) for s in range(N_SECTIONS)]
                 + [vec_spec(lb_logits.shape[0]), vec_spec(1), vec_spec(CONV_K)]
                 + cast_specs,
        out_specs=[out_spec, out_spec] + cast_specs,
        out_shape=[jax.ShapeDtypeStruct((m, width), BF16)] * 2 + cast_shapes,
        scratch_shapes=[pltpu.VMEM((2,) + slab_proj[0], F32),
                        pltpu.VMEM((PAIR, HEAD_DIM, HEAD_DIM), F32),
                        pltpu.VMEM((PAIR, SUBLANES, LANES), F32)],
        compiler_params=pltpu.CompilerParams(
            dimension_semantics=("arbitrary", "arbitrary"),
            vmem_limit_bytes=_vmem_limit(windows, resident)),
        name="mixer",
    )(hn, hn, *([w_in] * N_SECTIONS), lb_logits, out_norm_w, conv_w, *cast_weights)


def _outproj_kernel(oa_ref, ob_ref, x_ref, wa_ref, wb_ref, o_ref):
    o_ref[...] = (x_ref[...]
                  + jnp.dot(oa_ref[...], wa_ref[...], preferred_element_type=F32)
                  + jnp.dot(ob_ref[...], wb_ref[...], preferred_element_type=F32))


def _out_proj(oa, ob, x, w_out, *, tm=512):
    m, width = oa.shape
    d_mix, d = w_out.shape
    assert d_mix == 2 * width and m % tm == 0
    return pl.pallas_call(
        _outproj_kernel,
        grid=(m // tm,),
        in_specs=[
            pl.BlockSpec((tm, width), lambda i: (i, 0)),
            pl.BlockSpec((tm, width), lambda i: (i, 0)),
            pl.BlockSpec((tm, d), lambda i: (i, 0)),
            pl.BlockSpec((width, d), lambda i: (0, 0)),
            pl.BlockSpec((width, d), lambda i: (1, 0)),
        ],
        out_specs=pl.BlockSpec((tm, d), lambda i: (i, 0)),
        out_shape=jax.ShapeDtypeStruct((m, d), F32),
        compiler_params=pltpu.CompilerParams(
            dimension_semantics=("parallel",),
            vmem_limit_bytes=_vmem_limit(
                [((tm, width), BF16)] * 2 + [((tm, d), F32)] * 2 + [((width, d), BF16)] * 2, [])),
        name="out_proj",
    )(oa, ob, x, w_out, w_out)


def kernel(x, ffn1_norm_w, ffn1_w_gate, ffn1_w_up, ffn1_w_down, mix_norm_w, w_in_mix,
           hgrn_lb_logits, hgrn_out_norm_w, conv_w, w_out_mix, ffn2_norm_w, ffn2_w_gate,
           ffn2_w_up, ffn2_w_down, final_norm_w):
    batch, seq, d = x.shape
    depth = ffn1_norm_w.shape[0]
    m = batch * seq
    heads = hgrn_out_norm_w.shape[1] // HEAD_DIM
    assert depth >= 1 and conv_w.shape[2] == heads * LANES
    final_w = final_norm_w.reshape(1, d)

    y = x.reshape(m, d)
    for l in range(depth):
        last = l == depth - 1
        y0, w1_gate, w1_up, w1_down = _ffn_head(
            y, ffn1_norm_w[l].reshape(1, d), ffn1_w_gate[l], ffn1_w_up[l], ffn1_w_down[l])
        y, hn, w_in, w2_gate, w2_up = _ffn(
            y, ffn1_norm_w[l].reshape(1, d), w1_gate, w1_up, w1_down,
            mix_norm_w[l].reshape(1, d), post="norm_copy", first_tile=y0,
            cast_weights=(w_in_mix[l], ffn2_w_gate[l], ffn2_w_up[l]))
        oa, ob, w2_down, w_out = _mixer(
            hn, w_in, hgrn_lb_logits, hgrn_out_norm_w[l].reshape(1, -1), conv_w[l],
            layer=l, batch=batch, heads=heads, cast_weights=(ffn2_w_down[l], w_out_mix[l]))
        y = _out_proj(oa, ob, y, w_out)
        (y,) = _ffn(y, ffn2_norm_w[l].reshape(1, d), w2_gate, w2_up, w2_down, final_w,
                    post="norm" if last else None, norm_ahead=True)
    return y.reshape(batch, seq, d)
```

```python
import functools
import math

import jax
import jax.numpy as jnp
from jax import lax
from jax.experimental import pallas as pl
from jax.experimental.pallas import tpu as pltpu

LANES = 128
SUBLANES = 8
CHUNK = 64
HEAD_DIM = 128
PAIR = 2
N_SECTIONS = 7
CONV_K = 3
CUMSUM_ROWS = 256
RMS_EPS = 1e-6
FFN_RESIDUAL_SCALE = 0.5
MIB = 1024 * 1024
BF16_SUBLANES = 16
V7X_VMEM_BYTES = 64 * MIB
VMEM_RESERVED_BYTES = 4 * MIB
VMEM_COMPILER_TEMP_BYTES = 8 * MIB
ROW_CHUNK = 16
ROW_CHUNK_UNROLL = 16
AHEAD_CHUNKS = 8

F32 = jnp.float32
BF16 = jnp.bfloat16


def _rmsnorm_rows(x, w):
    return x * lax.rsqrt(jnp.mean(x * x, axis=-1, keepdims=True) + RMS_EPS) * w


def _for_row_chunks(n_rows, fn):
    def body(r, carry):
        fn(pl.ds(pl.multiple_of(r * ROW_CHUNK, ROW_CHUNK), ROW_CHUNK))
        return carry
    lax.fori_loop(0, n_rows // ROW_CHUNK, body, 0, unroll=ROW_CHUNK_UNROLL)


def _sigmoid(x):
    return 1.0 / (1.0 + jnp.exp(-x))


def _nbytes(shape, dtype):
    return math.prod(shape) * jnp.dtype(dtype).itemsize


def _vmem_limit(pipelined, resident):
    estimate = (2 * sum(_nbytes(s, t) for s, t in pipelined)
                + sum(_nbytes(s, t) for s, t in resident) + VMEM_COMPILER_TEMP_BYTES)
    return min(estimate, V7X_VMEM_BYTES - VMEM_RESERVED_BYTES)


def _cast_rows_per_block(rows, n_steps):
    rpb = BF16_SUBLANES
    while rows % rpb or rows // rpb > n_steps:
        rpb += BF16_SUBLANES
        assert rpb <= rows, "no row blocking of this weight fits the grid"
    return rpb


def _cast_specs(weights, grid, first_row=0):
    n_inner = math.prod(grid[1:])
    specs, shapes = [], []
    for w in weights:
        rows, cols = w.shape
        rpb = _cast_rows_per_block(rows, (grid[0] - first_row) * n_inner)
        last = rows // rpb - 1

        def index_map(i, *inner, last=last):
            step = (i - first_row) * n_inner + (inner[0] if inner else 0)
            return (jnp.clip(step, 0, last), 0)

        specs.append(pl.BlockSpec((rpb, cols), index_map))
        shapes.append(jax.ShapeDtypeStruct((rows, cols), BF16))
    return specs, shapes


def _run_casts(src_refs, dst_refs):
    for src, dst in zip(src_refs, dst_refs):
        dst[...] = src[...].astype(dst.dtype)


FFN_DMA_SEMAPHORES = [pltpu.SemaphoreType.DMA(())] * 2

def _ffn_kernel(x_hbm, nw_ref, wg_ref, wu_ref, wd_ref, pw_ref, *rest, post, n_cast, own_cast,
                has_first, norm_ahead):
    rest = list(rest)
    y0_hbm = rest.pop(0) if has_first else None
    n_out = 2 if post == "norm_copy" else 1
    n_w16 = 3 if own_cast else 0
    cast_src, rest = rest[:n_cast], rest[n_cast:]
    outs, rest = rest[:n_out], rest[n_out:]
    w16_refs, rest = rest[:n_w16], rest[n_w16:]
    cast_dst, rest = rest[:n_cast], rest[n_cast:]
    h_ref, r_ref, x_buf, x_sem, o_sem = rest
    o_ref = outs[0]
    i, j = pl.program_id(0), pl.program_id(1)
    tm, d = x_buf.shape
    first = 1 if has_first else 0
    computed = i >= first

    def x_copy(tile):
        return pltpu.make_async_copy(x_hbm.at[pl.ds(tile * tm, tm), :], x_buf, x_sem)

    if has_first:
        y0_copy = pltpu.make_async_copy(y0_hbm, o_ref, o_sem)

        @pl.when((i == 0) & (j == 0))
        def _():
            y0_copy.start()

        @pl.when((i == 0) & (j == 1))
        def _():
            y0_copy.wait()

    @pl.when((i == 0) & (j == 0))
    def _():
        x_copy(first).start()

    n_i, n_j = pl.num_programs(0), pl.num_programs(1)
    has_next = i + 1 < n_i
    if norm_ahead:
        slot = i % 2
        chunk_rows = tm // AHEAD_CHUNKS
        first_chunk_step = n_j - AHEAD_CHUNKS

        @pl.when((i == 0) & (j == 0))
        def _():
            x_copy(0).wait()

            def norm_rows(rows):
                h_ref[0, rows, :] = _rmsnorm_rows(x_buf[rows, :], nw_ref[...]).astype(BF16)
            _for_row_chunks(tm, norm_rows)

        @pl.when((j == 1) & has_next)
        def _():
            x_copy(i + 1).start()

        @pl.when((j == first_chunk_step) & has_next)
        def _():
            x_copy(i + 1).wait()
    else:
        @pl.when(computed & (j == 0))
        def _():
            x_copy(i).wait()

            def norm_rows(rows):
                h_ref[rows, :] = _rmsnorm_rows(x_buf[rows, :], nw_ref[...]).astype(BF16)
            _for_row_chunks(tm, norm_rows)

        @pl.when(computed & (j == n_j // 2) & has_next)
        def _():
            x_copy(i + 1).start()

    def ffn_step(normalise_chunk=False, first_step=False):
        if own_cast:
            wg, wu, wd = (w_ref[...].astype(BF16) for w_ref in (wg_ref, wu_ref, wd_ref))
            for w16_ref, w in zip(w16_refs, (wg, wu, wd)):
                w16_ref[...] = w
        else:
            wg, wu, wd = wg_ref[...], wu_ref[...], wd_ref[...]
        h = h_ref[slot] if norm_ahead else h_ref[...]
        g = jnp.dot(h, wg, preferred_element_type=F32)
        u = jnp.dot(h, wu, preferred_element_type=F32)
        _run_casts(cast_src, cast_dst)
        a = (g * _sigmoid(g)) * (u * FFN_RESIDUAL_SCALE)
        down = jnp.dot(a.astype(BF16), wd, preferred_element_type=F32)
        if first_step:
            o_ref[...] = x_buf[...] + down
        else:
            o_ref[...] += down
        if normalise_chunk:
            rows = pl.ds(pl.multiple_of((j - first_chunk_step) * chunk_rows, chunk_rows),
                         chunk_rows)
            h_ref[1 - slot, rows, :] = _rmsnorm_rows(x_buf[rows, :], nw_ref[...]).astype(BF16)

    pl.when(computed & (j == 0))(functools.partial(ffn_step, first_step=True))
    if norm_ahead:
        pl.when((j > 0) & (j < first_chunk_step))(ffn_step)
        pl.when(j >= first_chunk_step)(functools.partial(ffn_step, normalise_chunk=True))
    else:
        pl.when(computed & (j > 0))(ffn_step)

    if post is not None:
        @pl.when(j == pl.num_programs(1) - 1)
        def _():
            if post == "norm_copy":
                def copy_rows(rows):
                    outs[1][rows, :] = _rmsnorm_rows(o_ref[rows, :], pw_ref[...]).astype(BF16)
                _for_row_chunks(tm, copy_rows)
            else:
                def row_stats(rows):
                    y = o_ref[rows, :]
                    ms = jnp.mean(y * y, axis=-1, keepdims=True)
                    r_ref[rows, :] = jnp.broadcast_to(lax.rsqrt(ms + RMS_EPS),
                                                      (ROW_CHUNK, LANES))
                _for_row_chunks(tm, row_stats)

                def scale_rows(rows):
                    r = r_ref[rows, :]
                    o_ref[rows, :] = (o_ref[rows, :] * jnp.concatenate([r] * (d // LANES), axis=1)
                                      * pw_ref[...])
                _for_row_chunks(tm, scale_rows)


def _ffn(x, norm_w, w_gate, w_up, w_down, post_w, *, post, cast_weights=(), first_tile=None,
         norm_ahead=False, tm=1024, tf=512):
    m, d = x.shape
    d_ff = w_gate.shape[1]
    assert m % tm == 0 and d_ff % tf == 0 and post in (None, "norm", "norm_copy")
    grid = (m // tm, d_ff // tf)
    assert grid[1] >= 2, "the x prefetch is started in a later step of each row tile"
    has_first = first_tile is not None
    first = 1 if has_first else 0
    assert not (norm_ahead and has_first)
    assert not norm_ahead or (grid[1] >= AHEAD_CHUNKS + 2 and tm % (AHEAD_CHUNKS * ROW_CHUNK) == 0)
    h_shape = (2, tm, d) if norm_ahead else (tm, d)
    cast_specs, cast_shapes = _cast_specs(cast_weights, grid, first_row=first)
    row_spec = pl.BlockSpec((tm, d), lambda i, j: (i, 0))
    copy_specs = [row_spec] if post == "norm_copy" else []
    copy_shapes = [jax.ShapeDtypeStruct((m, d), BF16)] if post == "norm_copy" else []

    def col(i, j):
        return jnp.where(i < first, 0, j)

    windows = ([((tm, d), F32)] + [((tm, d), BF16)] * len(copy_specs) + [((d, tf), BF16)] * 3
               + [(s.block_shape, F32) for s in cast_specs]
               + [(s.block_shape, BF16) for s in cast_specs])
    scratch = [(h_shape, BF16), ((tm, LANES), F32), ((tm, d), F32)]
    return pl.pallas_call(
        functools.partial(_ffn_kernel, post=post, n_cast=len(cast_specs), own_cast=False,
                          has_first=has_first, norm_ahead=norm_ahead),
        grid=grid,
        in_specs=[
            pl.BlockSpec(memory_space=pl.ANY),
            pl.BlockSpec((1, d), lambda i, j: (0, 0)),
            pl.BlockSpec((d, tf), lambda i, j: (0, col(i, j))),
            pl.BlockSpec((d, tf), lambda i, j: (0, col(i, j))),
            pl.BlockSpec((tf, d), lambda i, j: (col(i, j), 0)),
            pl.BlockSpec((1, d), lambda i, j: (0, 0)),
        ] + [pl.BlockSpec(memory_space=pl.ANY)] * first + cast_specs,
        out_specs=[row_spec] + copy_specs + cast_specs,
        out_shape=[jax.ShapeDtypeStruct((m, d), F32)] + copy_shapes + cast_shapes,
        scratch_shapes=[pltpu.VMEM(s, t) for s, t in scratch] + FFN_DMA_SEMAPHORES,
        compiler_params=pltpu.CompilerParams(
            dimension_semantics=("arbitrary", "arbitrary"),
            vmem_limit_bytes=_vmem_limit(windows, scratch)),
        name="ffn",
    )(x, norm_w, w_gate, w_up, w_down, post_w, *([first_tile] * first), *cast_weights)


def _ffn_head(x, norm_w, w_gate, w_up, w_down, *, tm=1024, tf=256):
    m, d = x.shape
    d_ff = w_gate.shape[1]
    assert m % tm == 0 and d_ff % tf == 0
    grid = (1, d_ff // tf)
    specs = [pl.BlockSpec((d, tf), lambda i, j: (0, j)),
             pl.BlockSpec((d, tf), lambda i, j: (0, j)),
             pl.BlockSpec((tf, d), lambda i, j: (j, 0))]
    windows = [((d, tf), F32)] * 3 + [((d, tf), BF16)] * 3
    tile_spec = pl.BlockSpec((tm, d), lambda i, j: (0, 0), pipeline_mode=pl.Buffered(1))
    scratch = [((tm, d), BF16), ((tm, LANES), F32), ((tm, d), F32)]
    return pl.pallas_call(
        functools.partial(_ffn_kernel, post=None, n_cast=0, own_cast=True, has_first=False,
                          norm_ahead=False),
        grid=grid,
        in_specs=[pl.BlockSpec(memory_space=pl.ANY),
                  pl.BlockSpec((1, d), lambda i, j: (0, 0))] + specs
                 + [pl.BlockSpec((1, d), lambda i, j: (0, 0))],
        out_specs=[tile_spec] + specs,
        out_shape=[jax.ShapeDtypeStruct((tm, d), F32)]
                  + [jax.ShapeDtypeStruct(w.shape, BF16) for w in (w_gate, w_up, w_down)],
        scratch_shapes=[pltpu.VMEM(s, t) for s, t in scratch] + FFN_DMA_SEMAPHORES,
        compiler_params=pltpu.CompilerParams(
            dimension_semantics=("arbitrary", "arbitrary"),
            vmem_limit_bytes=_vmem_limit(windows, scratch + [((tm, d), F32)])),
        name="ffn_head",
    )(x, norm_w, w_gate, w_up, w_down, norm_w)


def _mixer_kernel(hn_ref, hnx_ref, wq_ref, wf_ref, wi_ref, wg_ref, wcb_ref, wcc_ref, wch_ref,
                  lb_ref, nw_ref, cw_ref, *rest, layer, group, n_cast):
    cast_src, (oa_ref, ob_ref, *cast_dst), (p_ref, st_ref, halo_ref) = (
        rest[:n_cast], rest[n_cast:2 * n_cast + 2], rest[2 * n_cast + 2:])
    _mixer_body(hn_ref, hnx_ref, (wq_ref, wf_ref, wi_ref, wg_ref, wcb_ref, wcc_ref, wch_ref),
                lb_ref, nw_ref, cw_ref, cast_src, oa_ref, ob_ref, cast_dst, p_ref, st_ref,
                halo_ref, layer=layer, group=group)


def _mixer_body(hn_ref, hnx_ref, w_refs, lb_ref, nw_ref, cw_ref, cast_src, oa_ref, ob_ref,
                cast_dst, p_ref, st_ref, halo_ref, *, layer, group):
    seq = hn_ref.shape[0]
    slab = group * CHUNK
    n_slabs = seq // slab
    SEC_Q, SEC_F, SEC_I, SEC_G, SEC_CB, SEC_CC, SEC_CH = range(N_SECTIONS)

    logits = lb_ref[...]
    e = jnp.exp(logits - jnp.max(logits, axis=0, keepdims=True))
    lb_all = jnp.sum(e[:layer + 1], axis=0, keepdims=True) / jnp.sum(e, axis=0, keepdims=True)
    nw_all = nw_ref[...]

    causal = (lax.broadcasted_iota(jnp.int32, (CHUNK, CHUNK), 0)
              >= lax.broadcasted_iota(jnp.int32, (CHUNK, CHUNK), 1))
    srow_i = lax.broadcasted_iota(jnp.int32, (CUMSUM_ROWS, CUMSUM_ROWS), 0)
    scol_i = lax.broadcasted_iota(jnp.int32, (CUMSUM_ROWS, CUMSUM_ROWS), 1)
    tril16 = ((srow_i >= scol_i) & (srow_i // CHUNK == scol_i // CHUNK)).astype(BF16)
    top_row = lax.broadcasted_iota(jnp.int32, (SUBLANES, LANES), 0)
    heads_in_step = range(PAIR)
    lanes = [slice(hh * HEAD_DIM, (hh + 1) * HEAD_DIM) for hh in heads_in_step]
    chunks = [slice(c * CHUNK, (c + 1) * CHUNK) for c in range(group)]

    def project_tasks(load_rows, slot):
        def task(s):
            p_ref[slot, s] = jnp.dot(load_rows(), w_refs[s][...], preferred_element_type=F32)
        return [functools.partial(task, s) for s in range(N_SECTIONS)]

    def mix_tasks(row0, slot):
        v = {}
        rows = pl.ds(row0, slab)

        def sec(s, hh):
            return p_ref[slot, s, :, lanes[hh]]

        def gates():
            v["hl"], v["k"] = [], []
            for hh in heads_in_step:
                lb = lb_all[:, lanes[hh]]
                f = lb + (1.0 - lb) * _sigmoid(sec(SEC_F, hh))
                log_f = jnp.log(f)
                v["k"].append(1.0 - f)
                hi = log_f.astype(BF16)
                lo = (log_f - hi.astype(F32)).astype(BF16)
                v["hl"].append(jnp.concatenate([hi, lo], axis=1))

        def cumsum():
            v["b"] = []
            for hh in heads_in_step:
                bb = jnp.concatenate(
                    [jnp.dot(tril16, v["hl"][hh][r0:r0 + CUMSUM_ROWS],
                             preferred_element_type=F32)
                     for r0 in range(0, slab, CUMSUM_ROWS)], axis=0)
                v["b"].append(bb[:, :HEAD_DIM] + bb[:, HEAD_DIM:])

        def decays():
            v["q_dec"], v["k_dec"], v["k_state"], v["decay"], v["v16"] = [], [], [], [], []
            for hh in heads_in_step:
                b = v["b"][hh]
                eb = jnp.exp(b)
                v["q_dec"].append((sec(SEC_Q, hh) * eb).astype(BF16))
                k_dec32 = v["k"][hh] * (1.0 / eb)
                v["k_dec"].append(k_dec32.astype(BF16))
                dec = [jnp.exp(b[ch.stop - 1:ch.stop, :]) for ch in chunks]
                v["decay"].append(dec)
                v["k_state"].append([(k_dec32[ch] * d).astype(BF16)
                                     for ch, d in zip(chunks, dec)])
                v["v16"].append(sec(SEC_I, hh).astype(BF16))

        def chunk_dots():
            v["scores"], v["update"] = [], []
            for hh in heads_in_step:
                v32 = sec(SEC_I, hh)
                v["scores"].append([lax.dot_general(v["q_dec"][hh][ch], v["k_dec"][hh][ch],
                                                    (((1,), (1,)), ((), ())),
                                                    preferred_element_type=F32)
                                    for ch in chunks])
                v["update"].append([jnp.dot(v32[ch].T.astype(BF16), v["k_state"][hh][c],
                                            preferred_element_type=F32)
                                    for c, ch in enumerate(chunks)])

        def read_out():
            v["o"] = []
            for hh in heads_in_step:
                o_intra = [jnp.dot(jnp.where(causal, v["scores"][hh][c], 0.0).astype(BF16),
                                   v["v16"][hh][ch], preferred_element_type=F32)
                           for c, ch in enumerate(chunks)]
                st_t = st_ref[hh]
                o_inter = []
                for c, ch in enumerate(chunks):
                    o_inter.append(lax.dot_general(v["q_dec"][hh][ch], st_t.astype(BF16),
                                                   (((1,), (1,)), ((), ())),
                                                   preferred_element_type=F32))
                    st_t = st_t * v["decay"][hh][c] + v["update"][hh][c]
                st_ref[hh] = st_t
                v["o"].append(jnp.concatenate([a + b for a, b in zip(o_intra, o_inter)], axis=0))

        def norm_gate():
            for hh in heads_in_step:
                o = v["o"][hh]
                gate = sec(SEC_G, hh)
                o = (o * lax.rsqrt(jnp.mean(o * o, axis=-1, keepdims=True) + RMS_EPS)
                     * nw_all[:, lanes[hh]])
                o = o * (gate * _sigmoid(gate))
                oa_ref[rows, lanes[hh]] = o.astype(oa_ref.dtype)

        def conv():
            for hh in heads_in_step:
                cw = cw_ref[:, lanes[hh]]
                u_in = sec(SEC_CC, hh) * sec(SEC_CH, hh)
                halo = halo_ref[hh]
                y = cw[CONV_K - 1:CONV_K, :] * u_in
                for tap in range(CONV_K - 1):
                    shift = CONV_K - 1 - tap
                    shifted = pltpu.roll(u_in, shift, axis=0)
                    top = shifted[:SUBLANES]
                    for r0 in range(shift):
                        hrow = SUBLANES - shift + r0
                        top = jnp.where(top_row == r0, halo[hrow:hrow + 1, :], top)
                    shifted = jnp.concatenate([top, shifted[SUBLANES:]], axis=0)
                    y = y + cw[tap:tap + 1, :] * shifted
                halo_ref[hh] = u_in[slab - SUBLANES:, :]
                ob_ref[rows, lanes[hh]] = (sec(SEC_CB, hh) * y).astype(ob_ref.dtype)

        return [gates, cumsum, decays, chunk_dots, read_out, norm_gate, conv]

    def run_interleaved(mix_list, proj_list):
        n = max(len(mix_list), len(proj_list))
        for t in range(n):
            for tasks in (proj_list, mix_list):
                for task in tasks[t * len(tasks) // n:(t + 1) * len(tasks) // n]:
                    task()

    @pl.when(pl.program_id(1) == 0)
    def _():
        for task in project_tasks(lambda: hn_ref[0:slab, :], 0):
            task()

    st_ref[...] = jnp.zeros_like(st_ref)
    halo_ref[...] = jnp.zeros_like(halo_ref)

    def slab_pair(i, carry):
        row0 = pl.multiple_of(i * (2 * slab), 2 * slab)
        run_interleaved(mix_tasks(row0, 0),
                        project_tasks(lambda: hn_ref[pl.ds(row0 + slab, slab), :], 1))
        run_interleaved(mix_tasks(row0 + slab, 1),
                        project_tasks(lambda: hn_ref[pl.ds(row0 + 2 * slab, slab), :], 0))
        return carry

    lax.fori_loop(0, n_slabs // 2 - 1, slab_pair, 0)
    row0 = (n_slabs - 2) * slab
    run_interleaved(mix_tasks(row0, 0),
                    project_tasks(lambda: hn_ref[row0 + slab:row0 + 2 * slab, :], 1))
    run_interleaved(mix_tasks(row0 + slab, 1), project_tasks(lambda: hnx_ref[...], 0))
    _run_casts(cast_src, cast_dst)


def _mixer(hn, w_in, lb_logits, out_norm_w, conv_w, *, layer, batch, heads, cast_weights=(),
           group=8):
    m, d = hn.shape
    seq = m // batch
    width = heads * HEAD_DIM
    pw = PAIR * HEAD_DIM
    n_pairs = heads // PAIR
    slab = group * CHUNK
    n_slabs = seq // slab
    assert heads % PAIR == 0 and w_in.shape == (d, N_SECTIONS * width)
    assert seq % slab == 0 and n_slabs % 2 == 0 and n_slabs >= 4 and CONV_K - 1 <= SUBLANES
    assert slab % CUMSUM_ROWS == 0 and CUMSUM_ROWS % CHUNK == 0

    def w_spec(section):
        return pl.BlockSpec((d, pw), lambda p, b, s=section: (0, s * n_pairs + p),
                            pipeline_mode=pl.Buffered(1))

    def vec_spec(rows):
        return pl.BlockSpec((rows, pw), lambda p, b: (0, p))

    grid = (n_pairs, batch)
    cast_specs, cast_shapes = _cast_specs(cast_weights, grid)
    out_spec = pl.BlockSpec((seq, pw), lambda p, b: (b, p))
    windows = ([((seq, d), BF16), ((slab, d), BF16)] + [((seq, pw), BF16)] * 2
               + [(s.block_shape, F32) for s in cast_specs]
               + [(s.block_shape, BF16) for s in cast_specs])
    slab_proj = ((N_SECTIONS, slab, pw), F32)
    resident = ([((d, pw), BF16)] * N_SECTIONS + [slab_proj] * 3
                + [((PAIR, HEAD_DIM, HEAD_DIM), F32), ((PAIR, SUBLANES, LANES), F32)])
    return pl.pallas_call(
        functools.partial(_mixer_kernel, layer=layer, group=group, n_cast=len(cast_specs)),
        grid=grid,
        in_specs=[pl.BlockSpec((seq, d), lambda p, b: (b, 0)),
                  pl.BlockSpec((slab, d), lambda p, b: (((b + 1) % batch) * n_slabs, 0))]
                 + [w_spec(s) for s in range(N_SECTIONS)]
                 + [vec_spec(lb_logits.shape[0]), vec_spec(1), vec_spec(CONV_K)]
                 + cast_specs,
        out_specs=[out_spec, out_spec] + cast_specs,
        out_shape=[jax.ShapeDtypeStruct((m, width), BF16)] * 2 + cast_shapes,
        scratch_shapes=[pltpu.VMEM((2,) + slab_proj[0], F32),
                        pltpu.VMEM((PAIR, HEAD_DIM, HEAD_DIM), F32),
                        pltpu.VMEM((PAIR, SUBLANES, LANES), F32)],
        compiler_params=pltpu.CompilerParams(
            dimension_semantics=("arbitrary", "arbitrary"),
            vmem_limit_bytes=_vmem_limit(windows, resident)),
        name="mixer",
    )(hn, hn, *([w_in] * N_SECTIONS), lb_logits, out_norm_w, conv_w, *cast_weights)


def _outproj_kernel(oa_ref, ob_ref, x_ref, wa_ref, wb_ref, o_ref):
    o_ref[...] = (x_ref[...]
                  + jnp.dot(oa_ref[...], wa_ref[...], preferred_element_type=F32)
                  + jnp.dot(ob_ref[...], wb_ref[...], preferred_element_type=F32))


def _out_proj(oa, ob, x, w_out, *, tm=512):
    m, width = oa.shape
    d_mix, d = w_out.shape
    assert d_mix == 2 * width and m % tm == 0
    return pl.pallas_call(
        _outproj_kernel,
        grid=(m // tm,),
        in_specs=[
            pl.BlockSpec((tm, width), lambda i: (i, 0)),
            pl.BlockSpec((tm, width), lambda i: (i, 0)),
            pl.BlockSpec((tm, d), lambda i: (i, 0)),
            pl.BlockSpec((width, d), lambda i: (0, 0)),
            pl.BlockSpec((width, d), lambda i: (1, 0)),
        ],
        out_specs=pl.BlockSpec((tm, d), lambda i: (i, 0)),
        out_shape=jax.ShapeDtypeStruct((m, d), F32),
        compiler_params=pltpu.CompilerParams(
            dimension_semantics=("parallel",),
            vmem_limit_bytes=_vmem_limit(
                [((tm, width), BF16)] * 2 + [((tm, d), F32)] * 2 + [((width, d), BF16)] * 2, [])),
        name="out_proj",
    )(oa, ob, x, w_out, w_out)


def kernel(x, ffn1_norm_w, ffn1_w_gate, ffn1_w_up, ffn1_w_down, mix_norm_w, w_in_mix,
           hgrn_lb_logits, hgrn_out_norm_w, conv_w, w_out_mix, ffn2_norm_w, ffn2_w_gate,
           ffn2_w_up, ffn2_w_down, final_norm_w):
    batch, seq, d = x.shape
    depth = ffn1_norm_w.shape[0]
    m = batch * seq
    heads = hgrn_out_norm_w.shape[1] // HEAD_DIM
    assert depth >= 1 and conv_w.shape[2] == heads * LANES
    final_w = final_norm_w.reshape(1, d)

    y = x.reshape(m, d)
    for l in range(depth):
        last = l == depth - 1
        y0, w1_gate, w1_up, w1_down = _ffn_head(
            y, ffn1_norm_w[l].reshape(1, d), ffn1_w_gate[l], ffn1_w_up[l], ffn1_w_down[l])
        y, hn, w_in, w2_gate, w2_up = _ffn(
            y, ffn1_norm_w[l].reshape(1, d), w1_gate, w1_up, w1_down,
            mix_norm_w[l].reshape(1, d), post="norm_copy", first_tile=y0,
            cast_weights=(w_in_mix[l], ffn2_w_gate[l], ffn2_w_up[l]))
        oa, ob, w2_down, w_out = _mixer(
            hn, w_in, hgrn_lb_logits, hgrn_out_norm_w[l].reshape(1, -1), conv_w[l],
            layer=l, batch=batch, heads=heads, cast_weights=(ffn2_w_down[l], w_out_mix[l]))
        y = _out_proj(oa, ob, y, w_out)
        (y,) = _ffn(y, ffn2_norm_w[l].reshape(1, d), w2_gate, w2_up, w2_down, final_w,
                    post="norm" if last else None, norm_ahead=True)
    return y.reshape(batch, seq, d)
```

```python
import functools
import math

import jax
import jax.numpy as jnp
from jax import lax
from jax.experimental import pallas as pl
from jax.experimental.pallas import tpu as pltpu

LANES = 128
SUBLANES = 8
CHUNK = 64
HEAD_DIM = 128
PAIR = 2
N_SECTIONS = 7
CONV_K = 3
CUMSUM_ROWS = 256
RMS_EPS = 1e-6
FFN_RESIDUAL_SCALE = 0.5
MIB = 1024 * 1024
BF16_SUBLANES = 16
V7X_VMEM_BYTES = 64 * MIB
VMEM_RESERVED_BYTES = 4 * MIB
VMEM_COMPILER_TEMP_BYTES = 8 * MIB
ROW_CHUNK = 16
ROW_CHUNK_UNROLL = 16
AHEAD_CHUNKS = 8

F32 = jnp.float32
BF16 = jnp.bfloat16


def _rmsnorm_rows(x, w):
    return x * lax.rsqrt(jnp.mean(x * x, axis=-1, keepdims=True) + RMS_EPS) * w


def _for_row_chunks(n_rows, fn):
    def body(r, carry):
        fn(pl.ds(pl.multiple_of(r * ROW_CHUNK, ROW_CHUNK), ROW_CHUNK))
        return carry
    lax.fori_loop(0, n_rows // ROW_CHUNK, body, 0, unroll=ROW_CHUNK_UNROLL)


def _sigmoid(x):
    return 1.0 / (1.0 + jnp.exp(-x))


def _nbytes(shape, dtype):
    return math.prod(shape) * jnp.dtype(dtype).itemsize


def _vmem_limit(pipelined, resident):
    estimate = (2 * sum(_nbytes(s, t) for s, t in pipelined)
                + sum(_nbytes(s, t) for s, t in resident) + VMEM_COMPILER_TEMP_BYTES)
    return min(estimate, V7X_VMEM_BYTES - VMEM_RESERVED_BYTES)


def _cast_rows_per_block(rows, n_steps):
    rpb = BF16_SUBLANES
    while rows % rpb or rows // rpb > n_steps:
        rpb += BF16_SUBLANES
        assert rpb <= rows, "no row blocking of this weight fits the grid"
    return rpb


def _cast_specs(weights, grid, first_row=0):
    n_inner = math.prod(grid[1:])
    specs, shapes = [], []
    for w in weights:
        rows, cols = w.shape
        rpb = _cast_rows_per_block(rows, (grid[0] - first_row) * n_inner)
        last = rows // rpb - 1

        def index_map(i, *inner, last=last):
            step = (i - first_row) * n_inner + (inner[0] if inner else 0)
            return (jnp.clip(step, 0, last), 0)

        specs.append(pl.BlockSpec((rpb, cols), index_map))
        shapes.append(jax.ShapeDtypeStruct((rows, cols), BF16))
    return specs, shapes


def _run_casts(src_refs, dst_refs):
    for src, dst in zip(src_refs, dst_refs):
        dst[...] = src[...].astype(dst.dtype)


FFN_DMA_SEMAPHORES = [pltpu.SemaphoreType.DMA(())] * 2
X_PREFETCH_DMA_PRIORITY = 1

def _ffn_kernel(x_hbm, nw_ref, wg_ref, wu_ref, wd_ref, pw_ref, *rest, post, n_cast, own_cast,
                has_first, norm_ahead):
    rest = list(rest)
    y0_hbm = rest.pop(0) if has_first else None
    n_out = 2 if post == "norm_copy" else 1
    n_w16 = 3 if own_cast else 0
    cast_src, rest = rest[:n_cast], rest[n_cast:]
    outs, rest = rest[:n_out], rest[n_out:]
    w16_refs, rest = rest[:n_w16], rest[n_w16:]
    cast_dst, rest = rest[:n_cast], rest[n_cast:]
    h_ref, r_ref, x_buf, x_sem, o_sem = rest
    o_ref = outs[0]
    i, j = pl.program_id(0), pl.program_id(1)
    tm, d = x_buf.shape
    first = 1 if has_first else 0
    computed = i >= first

    def x_copy(tile):
        return pltpu.make_async_copy(x_hbm.at[pl.ds(tile * tm, tm), :], x_buf, x_sem)

    if has_first:
        y0_copy = pltpu.make_async_copy(y0_hbm, o_ref, o_sem)

        @pl.when((i == 0) & (j == 0))
        def _():
            y0_copy.start()

        @pl.when((i == 0) & (j == 1))
        def _():
            y0_copy.wait()

    @pl.when((i == 0) & (j == 0))
    def _():
        x_copy(first).start()

    n_i, n_j = pl.num_programs(0), pl.num_programs(1)
    has_next = i + 1 < n_i
    if norm_ahead:
        slot = i % 2
        chunk_rows = tm // AHEAD_CHUNKS
        first_chunk_step = n_j - AHEAD_CHUNKS

        @pl.when((i == 0) & (j == 0))
        def _():
            x_copy(0).wait()

            def norm_rows(rows):
                h_ref[0, rows, :] = _rmsnorm_rows(x_buf[rows, :], nw_ref[...]).astype(BF16)
            _for_row_chunks(tm, norm_rows)

        @pl.when((j == 1) & has_next)
        def _():
            x_copy(i + 1).start(priority=X_PREFETCH_DMA_PRIORITY)

        @pl.when((j == first_chunk_step) & has_next)
        def _():
            x_copy(i + 1).wait()
    else:
        @pl.when(computed & (j == 0))
        def _():
            x_copy(i).wait()

            def norm_rows(rows):
                h_ref[rows, :] = _rmsnorm_rows(x_buf[rows, :], nw_ref[...]).astype(BF16)
            _for_row_chunks(tm, norm_rows)

        @pl.when(computed & (j == n_j // 2) & has_next)
        def _():
            x_copy(i + 1).start(priority=X_PREFETCH_DMA_PRIORITY)

    def ffn_step(normalise_chunk=False, first_step=False):
        if own_cast:
            wg, wu, wd = (w_ref[...].astype(BF16) for w_ref in (wg_ref, wu_ref, wd_ref))
            for w16_ref, w in zip(w16_refs, (wg, wu, wd)):
                w16_ref[...] = w
        else:
            wg, wu, wd = wg_ref[...], wu_ref[...], wd_ref[...]
        h = h_ref[slot] if norm_ahead else h_ref[...]
        g = jnp.dot(h, wg, preferred_element_type=F32)
        u = jnp.dot(h, wu, preferred_element_type=F32)
        _run_casts(cast_src, cast_dst)
        a = (g * _sigmoid(g)) * (u * FFN_RESIDUAL_SCALE)
        down = jnp.dot(a.astype(BF16), wd, preferred_element_type=F32)
        if first_step:
            o_ref[...] = x_buf[...] + down
        else:
            o_ref[...] += down
        if normalise_chunk:
            rows = pl.ds(pl.multiple_of((j - first_chunk_step) * chunk_rows, chunk_rows),
                         chunk_rows)
            h_ref[1 - slot, rows, :] = _rmsnorm_rows(x_buf[rows, :], nw_ref[...]).astype(BF16)

    pl.when(computed & (j == 0))(functools.partial(ffn_step, first_step=True))
    if norm_ahead:
        pl.when((j > 0) & (j < first_chunk_step))(ffn_step)
        pl.when(j >= first_chunk_step)(functools.partial(ffn_step, normalise_chunk=True))
    else:
        pl.when(computed & (j > 0))(ffn_step)

    if post is not None:
        @pl.when(j == pl.num_programs(1) - 1)
        def _():
            if post == "norm_copy":
                def copy_rows(rows):
                    outs[1][rows, :] = _rmsnorm_rows(o_ref[rows, :], pw_ref[...]).astype(BF16)
                _for_row_chunks(tm, copy_rows)
            else:
                def row_stats(rows):
                    y = o_ref[rows, :]
                    ms = jnp.mean(y * y, axis=-1, keepdims=True)
                    r_ref[rows, :] = jnp.broadcast_to(lax.rsqrt(ms + RMS_EPS),
                                                      (ROW_CHUNK, LANES))
                _for_row_chunks(tm, row_stats)

                def scale_rows(rows):
                    r = r_ref[rows, :]
                    o_ref[rows, :] = (o_ref[rows, :] * jnp.concatenate([r] * (d // LANES), axis=1)
                                      * pw_ref[...])
                _for_row_chunks(tm, scale_rows)


def _ffn(x, norm_w, w_gate, w_up, w_down, post_w, *, post, cast_weights=(), first_tile=None,
         norm_ahead=False, tm=1024, tf=512):
    m, d = x.shape
    d_ff = w_gate.shape[1]
    assert m % tm == 0 and d_ff % tf == 0 and post in (None, "norm", "norm_copy")
    grid = (m // tm, d_ff // tf)
    assert grid[1] >= 2, "the x prefetch is started in a later step of each row tile"
    has_first = first_tile is not None
    first = 1 if has_first else 0
    assert not (norm_ahead and has_first)
    assert not norm_ahead or (grid[1] >= AHEAD_CHUNKS + 2 and tm % (AHEAD_CHUNKS * ROW_CHUNK) == 0)
    h_shape = (2, tm, d) if norm_ahead else (tm, d)
    cast_specs, cast_shapes = _cast_specs(cast_weights, grid, first_row=first)
    row_spec = pl.BlockSpec((tm, d), lambda i, j: (i, 0))
    copy_specs = [row_spec] if post == "norm_copy" else []
    copy_shapes = [jax.ShapeDtypeStruct((m, d), BF16)] if post == "norm_copy" else []

    def col(i, j):
        return jnp.where(i < first, 0, j)

    windows = ([((tm, d), F32)] + [((tm, d), BF16)] * len(copy_specs) + [((d, tf), BF16)] * 3
               + [(s.block_shape, F32) for s in cast_specs]
               + [(s.block_shape, BF16) for s in cast_specs])
    scratch = [(h_shape, BF16), ((tm, LANES), F32), ((tm, d), F32)]
    return pl.pallas_call(
        functools.partial(_ffn_kernel, post=post, n_cast=len(cast_specs), own_cast=False,
                          has_first=has_first, norm_ahead=norm_ahead),
        grid=grid,
        in_specs=[
            pl.BlockSpec(memory_space=pl.ANY),
            pl.BlockSpec((1, d), lambda i, j: (0, 0)),
            pl.BlockSpec((d, tf), lambda i, j: (0, col(i, j))),
            pl.BlockSpec((d, tf), lambda i, j: (0, col(i, j))),
            pl.BlockSpec((tf, d), lambda i, j: (col(i, j), 0)),
            pl.BlockSpec((1, d), lambda i, j: (0, 0)),
        ] + [pl.BlockSpec(memory_space=pl.ANY)] * first + cast_specs,
        out_specs=[row_spec] + copy_specs + cast_specs,
        out_shape=[jax.ShapeDtypeStruct((m, d), F32)] + copy_shapes + cast_shapes,
        scratch_shapes=[pltpu.VMEM(s, t) for s, t in scratch] + FFN_DMA_SEMAPHORES,
        compiler_params=pltpu.CompilerParams(
            dimension_semantics=("arbitrary", "arbitrary"),
            vmem_limit_bytes=_vmem_limit(windows, scratch)),
        name="ffn",
    )(x, norm_w, w_gate, w_up, w_down, post_w, *([first_tile] * first), *cast_weights)


def _ffn_head(x, norm_w, w_gate, w_up, w_down, *, tm=1024, tf=256):
    m, d = x.shape
    d_ff = w_gate.shape[1]
    assert m % tm == 0 and d_ff % tf == 0
    grid = (1, d_ff // tf)
    specs = [pl.BlockSpec((d, tf), lambda i, j: (0, j)),
             pl.BlockSpec((d, tf), lambda i, j: (0, j)),
             pl.BlockSpec((tf, d), lambda i, j: (j, 0))]
    windows = [((d, tf), F32)] * 3 + [((d, tf), BF16)] * 3
    tile_spec = pl.BlockSpec((tm, d), lambda i, j: (0, 0), pipeline_mode=pl.Buffered(1))
    scratch = [((tm, d), BF16), ((tm, LANES), F32), ((tm, d), F32)]
    return pl.pallas_call(
        functools.partial(_ffn_kernel, post=None, n_cast=0, own_cast=True, has_first=False,
                          norm_ahead=False),
        grid=grid,
        in_specs=[pl.BlockSpec(memory_space=pl.ANY),
                  pl.BlockSpec((1, d), lambda i, j: (0, 0))] + specs
                 + [pl.BlockSpec((1, d), lambda i, j: (0, 0))],
        out_specs=[tile_spec] + specs,
        out_shape=[jax.ShapeDtypeStruct((tm, d), F32)]
                  + [jax.ShapeDtypeStruct(w.shape, BF16) for w in (w_gate, w_up, w_down)],
        scratch_shapes=[pltpu.VMEM(s, t) for s, t in scratch] + FFN_DMA_SEMAPHORES,
        compiler_params=pltpu.CompilerParams(
            dimension_semantics=("arbitrary", "arbitrary"),
            vmem_limit_bytes=_vmem_limit(windows, scratch + [((tm, d), F32)])),
        name="ffn_head",
    )(x, norm_w, w_gate, w_up, w_down, norm_w)


def _mixer_kernel(hn_ref, hnx_ref, wq_ref, wf_ref, wi_ref, wg_ref, wcb_ref, wcc_ref, wch_ref,
                  lb_ref, nw_ref, cw_ref, *rest, layer, group, n_cast):
    cast_src, (oa_ref, ob_ref, *cast_dst), (p_ref, st_ref, halo_ref) = (
        rest[:n_cast], rest[n_cast:2 * n_cast + 2], rest[2 * n_cast + 2:])
    _mixer_body(hn_ref, hnx_ref, (wq_ref, wf_ref, wi_ref, wg_ref, wcb_ref, wcc_ref, wch_ref),
                lb_ref, nw_ref, cw_ref, cast_src, oa_ref, ob_ref, cast_dst, p_ref, st_ref,
                halo_ref, layer=layer, group=group)


def _mixer_body(hn_ref, hnx_ref, w_refs, lb_ref, nw_ref, cw_ref, cast_src, oa_ref, ob_ref,
                cast_dst, p_ref, st_ref, halo_ref, *, layer, group):
    seq = hn_ref.shape[0]
    slab = group * CHUNK
    n_slabs = seq // slab
    SEC_Q, SEC_F, SEC_I, SEC_G, SEC_CB, SEC_CC, SEC_CH = range(N_SECTIONS)

    logits = lb_ref[...]
    e = jnp.exp(logits - jnp.max(logits, axis=0, keepdims=True))
    lb_all = jnp.sum(e[:layer + 1], axis=0, keepdims=True) / jnp.sum(e, axis=0, keepdims=True)
    nw_all = nw_ref[...]

    causal = (lax.broadcasted_iota(jnp.int32, (CHUNK, CHUNK), 0)
              >= lax.broadcasted_iota(jnp.int32, (CHUNK, CHUNK), 1))
    srow_i = lax.broadcasted_iota(jnp.int32, (CUMSUM_ROWS, CUMSUM_ROWS), 0)
    scol_i = lax.broadcasted_iota(jnp.int32, (CUMSUM_ROWS, CUMSUM_ROWS), 1)
    tril16 = ((srow_i >= scol_i) & (srow_i // CHUNK == scol_i // CHUNK)).astype(BF16)
    top_row = lax.broadcasted_iota(jnp.int32, (SUBLANES, LANES), 0)
    heads_in_step = range(PAIR)
    lanes = [slice(hh * HEAD_DIM, (hh + 1) * HEAD_DIM) for hh in heads_in_step]
    chunks = [slice(c * CHUNK, (c + 1) * CHUNK) for c in range(group)]

    def project_tasks(load_rows, slot):
        def task(s):
            p_ref[slot, s] = jnp.dot(load_rows(), w_refs[s][...], preferred_element_type=F32)
        return [functools.partial(task, s) for s in range(N_SECTIONS)]

    def mix_tasks(row0, slot):
        v = {}
        rows = pl.ds(row0, slab)

        def sec(s, hh):
            return p_ref[slot, s, :, lanes[hh]]

        def gates():
            v["hl"], v["k"] = [], []
            for hh in heads_in_step:
                lb = lb_all[:, lanes[hh]]
                f = lb + (1.0 - lb) * _sigmoid(sec(SEC_F, hh))
                log_f = jnp.log(f)
                v["k"].append(1.0 - f)
                hi = log_f.astype(BF16)
                lo = (log_f - hi.astype(F32)).astype(BF16)
                v["hl"].append(jnp.concatenate([hi, lo], axis=1))

        def cumsum():
            v["b"] = []
            for hh in heads_in_step:
                bb = jnp.concatenate(
                    [jnp.dot(tril16, v["hl"][hh][r0:r0 + CUMSUM_ROWS],
                             preferred_element_type=F32)
                     for r0 in range(0, slab, CUMSUM_ROWS)], axis=0)
                v["b"].append(bb[:, :HEAD_DIM] + bb[:, HEAD_DIM:])

        def decays():
            v["q_dec"], v["k_dec"], v["k_state"], v["decay"], v["v16"] = [], [], [], [], []
            for hh in heads_in_step:
                b = v["b"][hh]
                eb = jnp.exp(b)
                v["q_dec"].append((sec(SEC_Q, hh) * eb).astype(BF16))
                k_dec32 = v["k"][hh] * (1.0 / eb)
                v["k_dec"].append(k_dec32.astype(BF16))
                dec = [jnp.exp(b[ch.stop - 1:ch.stop, :]) for ch in chunks]
                v["decay"].append(dec)
                v["k_state"].append([(k_dec32[ch] * d).astype(BF16)
                                     for ch, d in zip(chunks, dec)])
                v["v16"].append(sec(SEC_I, hh).astype(BF16))

        def chunk_dots():
            v["scores"], v["update"] = [], []
            for hh in heads_in_step:
                v32 = sec(SEC_I, hh)
                v["scores"].append([lax.dot_general(v["q_dec"][hh][ch], v["k_dec"][hh][ch],
                                                    (((1,), (1,)), ((), ())),
                                                    preferred_element_type=F32)
                                    for ch in chunks])
                v["update"].append([jnp.dot(v32[ch].T.astype(BF16), v["k_state"][hh][c],
                                            preferred_element_type=F32)
                                    for c, ch in enumerate(chunks)])

        def read_out():
            v["o"] = []
            for hh in heads_in_step:
                o_intra = [jnp.dot(jnp.where(causal, v["scores"][hh][c], 0.0).astype(BF16),
                                   v["v16"][hh][ch], preferred_element_type=F32)
                           for c, ch in enumerate(chunks)]
                st_t = st_ref[hh]
                o_inter = []
                for c, ch in enumerate(chunks):
                    o_inter.append(lax.dot_general(v["q_dec"][hh][ch], st_t.astype(BF16),
                                                   (((1,), (1,)), ((), ())),
                                                   preferred_element_type=F32))
                    st_t = st_t * v["decay"][hh][c] + v["update"][hh][c]
                st_ref[hh] = st_t
                v["o"].append(jnp.concatenate([a + b for a, b in zip(o_intra, o_inter)], axis=0))

        def norm_gate():
            for hh in heads_in_step:
                o = v["o"][hh]
                gate = sec(SEC_G, hh)
                o = (o * lax.rsqrt(jnp.mean(o * o, axis=-1, keepdims=True) + RMS_EPS)
                     * nw_all[:, lanes[hh]])
                o = o * (gate * _sigmoid(gate))
                oa_ref[rows, lanes[hh]] = o.astype(oa_ref.dtype)

        def conv():
            for hh in heads_in_step:
                cw = cw_ref[:, lanes[hh]]
                u_in = sec(SEC_CC, hh) * sec(SEC_CH, hh)
                halo = halo_ref[hh]
                y = cw[CONV_K - 1:CONV_K, :] * u_in
                for tap in range(CONV_K - 1):
                    shift = CONV_K - 1 - tap
                    shifted = pltpu.roll(u_in, shift, axis=0)
                    top = shifted[:SUBLANES]
                    for r0 in range(shift):
                        hrow = SUBLANES - shift + r0
                        top = jnp.where(top_row == r0, halo[hrow:hrow + 1, :], top)
                    shifted = jnp.concatenate([top, shifted[SUBLANES:]], axis=0)
                    y = y + cw[tap:tap + 1, :] * shifted
                halo_ref[hh] = u_in[slab - SUBLANES:, :]
                ob_ref[rows, lanes[hh]] = (sec(SEC_CB, hh) * y).astype(ob_ref.dtype)

        return [gates, cumsum, decays, chunk_dots, read_out, norm_gate, conv]

    def run_interleaved(mix_list, proj_list):
        n = max(len(mix_list), len(proj_list))
        for t in range(n):
            for tasks in (proj_list, mix_list):
                for task in tasks[t * len(tasks) // n:(t + 1) * len(tasks) // n]:
                    task()

    @pl.when(pl.program_id(1) == 0)
    def _():
        for task in project_tasks(lambda: hn_ref[0:slab, :], 0):
            task()

    st_ref[...] = jnp.zeros_like(st_ref)
    halo_ref[...] = jnp.zeros_like(halo_ref)

    def slab_pair(i, carry):
        row0 = pl.multiple_of(i * (2 * slab), 2 * slab)
        run_interleaved(mix_tasks(row0, 0),
                        project_tasks(lambda: hn_ref[pl.ds(row0 + slab, slab), :], 1))
        run_interleaved(mix_tasks(row0 + slab, 1),
                        project_tasks(lambda: hn_ref[pl.ds(row0 + 2 * slab, slab), :], 0))
        return carry

    lax.fori_loop(0, n_slabs // 2 - 1, slab_pair, 0)
    row0 = (n_slabs - 2) * slab
    run_interleaved(mix_tasks(row0, 0),
                    project_tasks(lambda: hn_ref[row0 + slab:row0 + 2 * slab, :], 1))
    run_interleaved(mix_tasks(row0 + slab, 1), project_tasks(lambda: hnx_ref[...], 0))
    _run_casts(cast_src, cast_dst)


def _mixer(hn, w_in, lb_logits, out_norm_w, conv_w, *, layer, batch, heads, cast_weights=(),
           group=8):
    m, d = hn.shape
    seq = m // batch
    width = heads * HEAD_DIM
    pw = PAIR * HEAD_DIM
    n_pairs = heads // PAIR
    slab = group * CHUNK
    n_slabs = seq // slab
    assert heads % PAIR == 0 and w_in.shape == (d, N_SECTIONS * width)
    assert seq % slab == 0 and n_slabs % 2 == 0 and n_slabs >= 4 and CONV_K - 1 <= SUBLANES
    assert slab % CUMSUM_ROWS == 0 and CUMSUM_ROWS % CHUNK == 0

    def w_spec(section):
        return pl.BlockSpec((d, pw), lambda p, b, s=section: (0, s * n_pairs + p),
                            pipeline_mode=pl.Buffered(1))

    def vec_spec(rows):
        return pl.BlockSpec((rows, pw), lambda p, b: (0, p))

    grid = (n_pairs, batch)
    cast_specs, cast_shapes = _cast_specs(cast_weights, grid)
    out_spec = pl.BlockSpec((seq, pw), lambda p, b: (b, p))
    windows = ([((seq, d), BF16), ((slab, d), BF16)] + [((seq, pw), BF16)] * 2
               + [(s.block_shape, F32) for s in cast_specs]
               + [(s.block_shape, BF16) for s in cast_specs])
    slab_proj = ((N_SECTIONS, slab, pw), F32)
    resident = ([((d, pw), BF16)] * N_SECTIONS + [slab_proj] * 3
                + [((PAIR, HEAD_DIM, HEAD_DIM), F32), ((PAIR, SUBLANES, LANES), F32)])
    return pl.pallas_call(
        functools.partial(_mixer_kernel, layer=layer, group=group, n_cast=len(cast_specs)),
        grid=grid,
        in_specs=[pl.BlockSpec((seq, d), lambda p, b: (b, 0)),
                  pl.BlockSpec((slab, d), lambda p, b: (((b + 1) % batch) * n_slabs, 0))]
                 + [w_spec(s) for s in range(N_SECTIONS)]
                 + [vec_spec(lb_logits.shape[0]), vec_spec(1), vec_spec(CONV_K)]
                 + cast_specs,
        out_specs=[out_spec, out_spec] + cast_specs,
        out_shape=[jax.ShapeDtypeStruct((m, width), BF16)] * 2 + cast_shapes,
        scratch_shapes=[pltpu.VMEM((2,) + slab_proj[0], F32),
                        pltpu.VMEM((PAIR, HEAD_DIM, HEAD_DIM), F32),
                        pltpu.VMEM((PAIR, SUBLANES, LANES), F32)],
        compiler_params=pltpu.CompilerParams(
            dimension_semantics=("arbitrary", "arbitrary"),
            vmem_limit_bytes=_vmem_limit(windows, resident)),
        name="mixer",
    )(hn, hn, *([w_in] * N_SECTIONS), lb_logits, out_norm_w, conv_w, *cast_weights)


def _outproj_kernel(oa_ref, ob_ref, x_ref, wa_ref, wb_ref, o_ref):
    o_ref[...] = (x_ref[...]
                  + jnp.dot(oa_ref[...], wa_ref[...], preferred_element_type=F32)
                  + jnp.dot(ob_ref[...], wb_ref[...], preferred_element_type=F32))


def _out_proj(oa, ob, x, w_out, *, tm=512):
    m, width = oa.shape
    d_mix, d = w_out.shape
    assert d_mix == 2 * width and m % tm == 0
    return pl.pallas_call(
        _outproj_kernel,
        grid=(m // tm,),
        in_specs=[
            pl.BlockSpec((tm, width), lambda i: (i, 0)),
            pl.BlockSpec((tm, width), lambda i: (i, 0)),
            pl.BlockSpec((tm, d), lambda i: (i, 0)),
            pl.BlockSpec((width, d), lambda i: (0, 0)),
            pl.BlockSpec((width, d), lambda i: (1, 0)),
        ],
        out_specs=pl.BlockSpec((tm, d), lambda i: (i, 0)),
        out_shape=jax.ShapeDtypeStruct((m, d), F32),
        compiler_params=pltpu.CompilerParams(
            dimension_semantics=("parallel",),
            vmem_limit_bytes=_vmem_limit(
                [((tm, width), BF16)] * 2 + [((tm, d), F32)] * 2 + [((width, d), BF16)] * 2, [])),
        name="out_proj",
    )(oa, ob, x, w_out, w_out)


def kernel(x, ffn1_norm_w, ffn1_w_gate, ffn1_w_up, ffn1_w_down, mix_norm_w, w_in_mix,
           hgrn_lb_logits, hgrn_out_norm_w, conv_w, w_out_mix, ffn2_norm_w, ffn2_w_gate,
           ffn2_w_up, ffn2_w_down, final_norm_w):
    batch, seq, d = x.shape
    depth = ffn1_norm_w.shape[0]
    m = batch * seq
    heads = hgrn_out_norm_w.shape[1] // HEAD_DIM
    assert depth >= 1 and conv_w.shape[2] == heads * LANES
    final_w = final_norm_w.reshape(1, d)

    y = x.reshape(m, d)
    for l in range(depth):
        last = l == depth - 1
        y0, w1_gate, w1_up, w1_down = _ffn_head(
            y, ffn1_norm_w[l].reshape(1, d), ffn1_w_gate[l], ffn1_w_up[l], ffn1_w_down[l])
        y, hn, w_in, w2_gate, w2_up = _ffn(
            y, ffn1_norm_w[l].reshape(1, d), w1_gate, w1_up, w1_down,
            mix_norm_w[l].reshape(1, d), post="norm_copy", first_tile=y0,
            cast_weights=(w_in_mix[l], ffn2_w_gate[l], ffn2_w_up[l]))
        oa, ob, w2_down, w_out = _mixer(
            hn, w_in, hgrn_lb_logits, hgrn_out_norm_w[l].reshape(1, -1), conv_w[l],
            layer=l, batch=batch, heads=heads, cast_weights=(ffn2_w_down[l], w_out_mix[l]))
        y = _out_proj(oa, ob, y, w_out)
        (y,) = _ffn(y, ffn2_norm_w[l].reshape(1, d), w2_gate, w2_up, w2_down, final_w,
                    post="norm" if last else None, norm_ahead=True)
    return y.reshape(batch, seq, d)
```
